```python
import math
import jax
import jax.numpy as jnp
from jax import lax
import numpy as np

D_MODEL = 1024
BATCH = 16
SEQ = 2048
DEPTH = 4
DEC_BATCH = 128
DEC_SEQ = 4
PAST_LEN = 8192
PAGE_SIZE = 128

N_MIXERS = 3
N_A = len(range(0, DEPTH, N_MIXERS))
N_B = len(range(1, DEPTH, N_MIXERS))
N_C = len(range(2, DEPTH, N_MIXERS))

DN_ALPHA = (2 * DEPTH) ** 0.25
DN_BETA = (8 * DEPTH) ** -0.25
LN_EPS = 1e-5
RMS_EPS = 1e-6

A_WINDOWS = (128, 512, 2048)
A_DILATIONS = (1, 4, 16)
A_GROUPS = len(A_WINDOWS)
A_HEADS = 8
A_HEAD_DIM = 64

B_HEADS = 8
B_DK = 128
B_DV = 128
B_WIDTH = B_HEADS * B_DK
CONV_W = 4
B_CHUNK = 64

C_HEADS = 8
C_NOPE = 128
C_ROPE = 64
C_V = 128
C_KV_LORA = 256
C_Q_LORA = 384
C_SCALE = (C_NOPE + C_ROPE) ** -0.5
ROPE_THETA = 10000.0
Q_BLOCK = 128

P_HEADS = 8
N_KEYS = 128
N_EXPERTS = N_KEYS * N_KEYS
P_QDIM = 256
P_TOPK = 16
P_CHUNK = 256

kernel_name = 'hybrid_dilated_gdn_mla_peer_step'


def layer_norm(x, g, b):
    xf = x.astype(jnp.float32)
    mu = jnp.mean(xf, -1, keepdims=True)
    var = jnp.mean(jnp.square(xf - mu), -1, keepdims=True)
    return ((xf - mu) * lax.rsqrt(var + LN_EPS) * g + b).astype(x.dtype)


def rms_norm(x, g):
    xf = x.astype(jnp.float32)
    return (xf * lax.rsqrt(jnp.mean(xf * xf, -1, keepdims=True) + RMS_EPS) * g).astype(x.dtype)


def l2_normalize(x):
    xf = x.astype(jnp.float32)
    return xf * lax.rsqrt(jnp.sum(xf * xf, -1, keepdims=True) + 1e-6)


def rope(x, pos):
    half = x.shape[-1] // 2
    inv = ROPE_THETA ** (-jnp.arange(half, dtype=jnp.float32) / half)
    ang = pos.astype(jnp.float32)[:, None] * inv[None, :]
    shape = (1, x.shape[1]) + (1,) * (x.ndim - 3) + (half,)
    cos, sin = jnp.cos(ang).reshape(shape), jnp.sin(ang).reshape(shape)
    xf = x.astype(jnp.float32)
    x1, x2 = xf[..., :half], xf[..., half:]
    return jnp.concatenate([x1 * cos - x2 * sin, x2 * cos + x1 * sin], -1).astype(x.dtype)


def dilated_attn_prompt(q, k, v, dilation, span):
    B, T, H, Dh = q.shape
    L = T // dilation
    qb = min(span, L)
    nb = -(-L // qb)
    pad = nb * qb - L

    def sub(x):
        x = x.reshape(B, L, dilation, H, Dh).transpose(0, 2, 1, 3, 4)
        x = jnp.pad(x, ((0, 0), (0, 0), (0, pad), (0, 0), (0, 0)))
        return x.reshape(B, dilation, nb, qb, H, Dh)

    def with_prev(x):
        prev = jnp.pad(x, ((0, 0), (0, 0), (1, 0), (0, 0), (0, 0), (0, 0)))[:, :, :-1]
        return jnp.concatenate([prev, x], axis=3)

    qs = sub(q)
    ks, vs = with_prev(sub(k)), with_prev(sub(v))
    qi = np.arange(nb)[:, None] * qb + np.arange(qb)[None, :]
    ki = np.arange(nb)[:, None] * qb - qb + np.arange(2 * qb)[None, :]
    dist = qi[:, :, None] - ki[:, None, :]
    valid = (dist >= 0) & (dist <= span) & (ki[:, None, :] >= 0)
    s = jnp.einsum('brnqhd,brnkhd->brnhqk', qs, ks).astype(jnp.float32) * (Dh ** -0.5)
    s = jnp.where(valid[None, None, :, None], s, -jnp.inf)
    lse = jax.nn.logsumexp(s, axis=-1)
    p = jnp.exp(s - lse[..., None]).astype(v.dtype)
    o = jnp.einsum('brnhqk,brnkhd->brnqhd', p, vs)
    o = o.reshape(B, dilation, nb * qb, H, Dh)[:, :, :L].transpose(0, 2, 1, 3, 4).reshape(B, T, H, Dh)
    lse = lse.transpose(0, 1, 2, 4, 3).reshape(B, dilation, nb * qb, H)[:, :, :L]
    lse = lse.transpose(0, 2, 1, 3).reshape(B, T, H)
    return o, lse


def dilated_attn_step(q, k, v, kv_buf, dilation, span):
    B, Ts, H, Dh = q.shape
    Lb = kv_buf.shape[1]
    kall = jnp.concatenate([kv_buf[:, :, 0], k], 1)
    vall = jnp.concatenate([kv_buf[:, :, 1], v], 1)
    idx = Lb + np.arange(Ts)[:, None] - dilation * np.arange(span + 1)[None, :]
    valid = idx >= 0
    idx = np.maximum(idx, 0)
    kg, vg = kall[:, idx], vall[:, idx]
    s = jnp.einsum('bqhd,bqmhd->bhqm', q, kg).astype(jnp.float32) * (Dh ** -0.5)
    s = jnp.where(valid[None, None], s, -jnp.inf)
    lse = jax.nn.logsumexp(s, axis=-1)
    p = jnp.exp(s - lse[..., None]).astype(v.dtype)
    o = jnp.einsum('bhqm,bqmhd->bqhd', p, vg)
    return o, lse.transpose(0, 2, 1)


def mixer_a(x, w_in, w_out, bufs):
    B, T, _ = x.shape
    proj = (x @ w_in).reshape(B, T, A_GROUPS, 3, A_HEADS, A_HEAD_DIM)
    outs, lses, rows = [], [], []
    for g in range(A_GROUPS):
        win, dil = A_WINDOWS[g], A_DILATIONS[g]
        q, k, v = proj[:, :, g, 0], proj[:, :, g, 1], proj[:, :, g, 2]
        span = win // dil
        if bufs is None:
            o, lse = dilated_attn_prompt(q, k, v, dil, span)
            keep = min(win, T)
            rows.append(jnp.stack([k[:, T - keep:], v[:, T - keep:]], axis=2))
        else:
            o, lse = dilated_attn_step(q, k, v, bufs[g], dil, span)
            rows.append(jnp.stack([k, v], axis=2))
        outs.append(o)
        lses.append(lse)
    wts = jax.nn.softmax(jnp.stack(lses, 0), axis=0).astype(x.dtype)
    o = jnp.sum(wts[..., None] * jnp.stack(outs, 0), 0)
    return o.reshape(B, T, A_HEADS * A_HEAD_DIM) @ w_out, rows


def causal_conv(x_pre, buf, w):
    T = x_pre.shape[1]
    xp = jnp.concatenate([buf, x_pre], 1)
    y = sum(xp[:, j:j + T] * w[j] for j in range(CONV_W))
    return y, xp[:, T:]


def gated_delta_rule(q, k, v, g, beta, s0):
    B, T, H, DK = q.shape
    DV = v.shape[-1]
    C = min(B_CHUNK, T)
    n = -(-T // C)
    pad = n * C - T

    def chunks(x):
        x = jnp.pad(x, ((0, 0), (0, pad)) + ((0, 0),) * (x.ndim - 2))
        x = x.reshape((B, n, C) + x.shape[2:])
        return jnp.swapaxes(jnp.moveaxis(x, 1, 0), 2, 3)

    qc, kc, vc, gc, bc = chunks(q), chunks(k), chunks(v), chunks(g), chunks(beta)
    gcum = jnp.cumsum(gc, axis=-1)
    kb, vb = kc * bc[..., None], vc * bc[..., None]
    tri = np.tril(np.ones((C, C), bool))
    stri = np.tril(np.ones((C, C), bool), -1)
    dmat = jnp.exp(jnp.where(tri, gcum[..., :, None] - gcum[..., None, :], -jnp.inf))
    a = jnp.where(stri, jnp.einsum('nbhid,nbhjd->nbhij', kb, kc) * dmat, 0.0)
    eye = jnp.eye(C, dtype=jnp.float32)
    tinv = lax.linalg.triangular_solve(eye + a, jnp.broadcast_to(eye, a.shape),
                                       left_side=True, lower=True, unit_diagonal=True)
    wk = tinv @ (kb * jnp.exp(gcum)[..., None])
    uv = tinv @ vb
    qk = jnp.where(tri, jnp.einsum('nbhid,nbhjd->nbhij', qc, kc) * dmat, 0.0)

    def step(S, xs):
        qi, ki, wi, ui, qki, gi = xs
        vnew = ui - wi @ S
        o = (qi * jnp.exp(gi)[..., None]) @ S + qki @ vnew
        glast = gi[..., -1]
        S = S * jnp.exp(glast)[..., None, None] + jnp.einsum(
            'bhcd,bhce->bhde', ki * jnp.exp(glast[..., None] - gi)[..., None], vnew)
        return S, o

    S, o = lax.scan(step, s0, (qc, kc, wk, uv, qk, gcum))
    o = o.transpose(1, 0, 3, 2, 4).reshape(B, n * C, H, DV)[:, :T]
    return o, S


def mixer_b(x, w_in, conv_w, a_log, dt_bias, norm_w, w_out, conv_buf, s0):
    B, T, _ = x.shape
    proj = x @ w_in
    qkv_pre = proj[..., :3 * B_WIDTH]
    z = proj[..., 3 * B_WIDTH:4 * B_WIDTH]
    b_in = proj[..., 4 * B_WIDTH:4 * B_WIDTH + B_HEADS]
    a_in = proj[..., 4 * B_WIDTH + B_HEADS:]
    if conv_buf is None:
        conv_buf = jnp.zeros((B, CONV_W - 1, 3 * B_WIDTH), x.dtype)
    if s0 is None:
        s0 = jnp.zeros((B, B_HEADS, B_DK, B_DV), jnp.float32)
    qkv, new_buf = causal_conv(qkv_pre, conv_buf, conv_w)
    qkv = jax.nn.silu(qkv)
    q = l2_normalize(qkv[..., :B_WIDTH].reshape(B, T, B_HEADS, B_DK)) * (B_DK ** -0.5)
    k = l2_normalize(qkv[..., B_WIDTH:2 * B_WIDTH].reshape(B, T, B_HEADS, B_DK))
    v = qkv[..., 2 * B_WIDTH:].reshape(B, T, B_HEADS, B_DV).astype(jnp.float32)
    beta = jax.nn.sigmoid(b_in.astype(jnp.float32))
    g = -jnp.exp(a_log.astype(jnp.float32)) * jax.nn.softplus(a_in.astype(jnp.float32) + dt_bias)
    o, S = gated_delta_rule(q, k, v, g, beta, s0.astype(jnp.float32))
    o = rms_norm(o, norm_w) * jax.nn.silu(z.reshape(B, T, B_HEADS, B_DV).astype(jnp.float32))
    y = o.astype(x.dtype).reshape(B, T, B_WIDTH) @ w_out
    return y, new_buf, S.astype(x.dtype)


def mla_attend(qcat, lat, q_start):
    B, T, H, Cd = qcat.shape
    S = lat.shape[1]
    qb = min(Q_BLOCK, T)
    nb = -(-T // qb)
    pad = nb * qb - T
    qblocks = jnp.swapaxes(jnp.pad(qcat, ((0, 0), (0, pad), (0, 0), (0, 0))).reshape(B, nb, qb, H, Cd), 0, 1)
    kpos = jnp.arange(S)
    ckv = lat[..., :C_KV_LORA]

    def block(args):
        qblk, bi = args
        qpos = q_start + bi * qb + jnp.arange(qb)
        s = jnp.einsum('bqhc,bsc->bhqs', qblk, lat).astype(jnp.float32) * C_SCALE
        s = jnp.where(kpos[None, :] <= qpos[:, None], s, -jnp.inf)
        p = jax.nn.softmax(s, axis=-1).astype(lat.dtype)
        return jnp.einsum('bhqs,bsr->bqhr', p, ckv)

    o = lax.map(block, (qblocks, jnp.arange(nb)))
    return jnp.swapaxes(o, 0, 1).reshape(B, nb * qb, H, C_KV_LORA)[:, :T]


def mixer_c(x, pos, w_in, q_norm, kv_norm, w_uq, w_uk, w_uv, w_out, lat_past):
    B, T, _ = x.shape
    proj = x @ w_in
    cq = rms_norm(proj[..., :C_Q_LORA], q_norm)
    ckv = rms_norm(proj[..., C_Q_LORA:C_Q_LORA + C_KV_LORA], kv_norm)
    kr = rope(proj[..., C_Q_LORA + C_KV_LORA:], pos)
    q = (cq @ w_uq).reshape(B, T, C_HEADS, C_NOPE + C_ROPE)
    q_rope = rope(q[..., C_NOPE:], pos)
    q_lat = jnp.einsum('bthn,rhn->bthr', q[..., :C_NOPE], w_uk)
    qcat = jnp.concatenate([q_lat, q_rope], -1)
    lat_new = jnp.concatenate([ckv, kr], -1)
    lat = lat_new if lat_past is None else jnp.concatenate([lat_past, lat_new], 1)
    o_lat = mla_attend(qcat, lat, lat.shape[1] - T)
    o = jnp.einsum('bthr,rhv->bthv', o_lat, w_uv).reshape(B, T, C_HEADS * C_V)
    return o @ w_out, lat_new


def peer_ffn(x, wq, keys, u_tab, v_tab):
    B, T, D = x.shape
    n = B * T
    cn = min(P_CHUNK, n)
    nc = -(-n // cn)
    xs = jnp.pad(x.reshape(n, D), ((0, nc * cn - n), (0, 0))).reshape(nc, cn, D)

    def chunk(xc):
        q = (xc @ wq).reshape(cn, P_HEADS, 2, P_QDIM // 2)
        s = jnp.einsum('nhcd,hckd->nhck', q, keys).astype(jnp.float32)
        s1, i1 = lax.top_k(s[:, :, 0], P_TOPK)
        s2, i2 = lax.top_k(s[:, :, 1], P_TOPK)
        cand = (s1[..., :, None] + s2[..., None, :]).reshape(cn, P_HEADS, P_TOPK * P_TOPK)
        cidx = (i1[..., :, None] * N_KEYS + i2[..., None, :]).reshape(cn, P_HEADS, P_TOPK * P_TOPK)
        top, sel = lax.top_k(cand, P_TOPK)
        eidx = jnp.take_along_axis(cidx, sel, -1)
        gate = jax.nn.softmax(top, axis=-1)
        h = jnp.einsum('nd,nhkd->nhk', xc, u_tab[eidx]).astype(jnp.float32)
        act = (jax.nn.gelu(h, approximate=False) * gate).astype(xc.dtype)
        return jnp.einsum('nhk,nhkd->nd', act, v_tab[eidx])

    y = lax.map(chunk, xs).reshape(nc * cn, D)[:n]
    return y.reshape(B, T, D)


def run_trunk(x, pos, prm, cache):
    new_a = [[] for _ in range(A_GROUPS)]
    new_conv, new_rec, new_lat = [], [], []
    for i in range(DEPTH):
        kind, li = i % N_MIXERS, i // N_MIXERS
        if kind == 0:
            bufs = None if cache is None else [w[li] for w in cache['a_win']]
            h, rows = mixer_a(x, prm['w_a_in'][li], prm['w_a_out'][li], bufs)
            for g in range(A_GROUPS):
                new_a[g].append(rows[g])
        elif kind == 1:
            conv_buf = None if cache is None else cache['b_conv'][li]
            s0 = None if cache is None else cache['b_rec'][li]
            h, cb, st = mixer_b(x, prm['w_b_in'][li], prm['b_conv_w'][li], prm['b_a_log'][li],
                                prm['b_dt_bias'][li], prm['b_norm_w'][li], prm['w_b_out'][li], conv_buf, s0)
            new_conv.append(cb)
            new_rec.append(st)
        else:
            if cache is None:
                lat_past = None
            else:
                pool, pt = cache['c_pool'][li], cache['page_table']
                lat_past = pool[pt].reshape(pt.shape[0], pt.shape[1] * pool.shape[1], pool.shape[2])
            h, lat = mixer_c(x, pos, prm['w_c_in'][li], prm['c_q_norm'][li], prm['c_kv_norm'][li],
                             prm['w_c_uq'][li], prm['w_c_uk'][li], prm['w_c_uv'][li], prm['w_c_out'][li], lat_past)
            new_lat.append(lat)
        x = layer_norm(DN_ALPHA * x + h, prm['ln1_g'][i], prm['ln1_b'][i])
        f = peer_ffn(x, prm['peer_wq'][i], prm['peer_keys'][i], prm['peer_u'][i], prm['peer_v'][i])
        x = layer_norm(DN_ALPHA * x + f, prm['ln2_g'][i], prm['ln2_b'][i])
    return x, [jnp.stack(r) for r in new_a], jnp.stack(new_conv), jnp.stack(new_rec), jnp.stack(new_lat)


def setup_inputs(seed: int = 0) -> dict:
    key = jax.random.key(seed)
    ks = iter(jax.random.split(key, 48))

    def nrm(shape, scale):
        return jax.random.normal(next(ks), shape, jnp.float32) * scale

    n_pages = PAST_LEN // PAGE_SIZE
    n_pool = (DEC_BATCH * n_pages * 5) // 4
    a_cols = A_GROUPS * 3 * A_HEADS * A_HEAD_DIM
    b_cols = 4 * B_WIDTH + 2 * B_HEADS
    c_cols = C_Q_LORA + C_KV_LORA + C_ROPE
    inp = {}
    inp['x_prompt'] = nrm((BATCH, SEQ, D_MODEL), 1.0)
    inp['x_sample'] = nrm((DEC_BATCH, DEC_SEQ, D_MODEL), 1.0)
    for g in range(A_GROUPS):
        inp['cache_a_win%d' % g] = nrm((N_A, DEC_BATCH, min(A_WINDOWS[g], PAST_LEN), 2, A_HEADS, A_HEAD_DIM), 1.0)
    inp['state_b_conv'] = nrm((N_B, DEC_BATCH, CONV_W - 1, 3 * B_WIDTH), 1.0)
    inp['state_b_rec'] = nrm((N_B, DEC_BATCH, B_HEADS, B_DK, B_DV), 0.1)
    inp['cache_c_latent'] = nrm((N_C, n_pool, PAGE_SIZE, C_KV_LORA + C_ROPE), 1.0)
    perm = jax.random.permutation(next(ks), n_pool)[:DEC_BATCH * n_pages]
    inp['page_table'] = perm.reshape(DEC_BATCH, n_pages).astype(jnp.int32)
    inp['w_a_in'] = nrm((N_A, D_MODEL, a_cols), D_MODEL ** -0.5)
    inp['w_a_out'] = nrm((N_A, A_HEADS * A_HEAD_DIM, D_MODEL), DN_BETA * (A_HEADS * A_HEAD_DIM) ** -0.5)
    inp['w_b_in'] = nrm((N_B, D_MODEL, b_cols), D_MODEL ** -0.5)
    inp['b_conv_w'] = nrm((N_B, CONV_W, 3 * B_WIDTH), CONV_W ** -0.5)
    inp['b_a_log'] = jnp.log(jax.random.uniform(next(ks), (N_B, B_HEADS), jnp.float32, 1.0, 16.0))
    dt = jnp.exp(jax.random.uniform(next(ks), (N_B, B_HEADS), jnp.float32, math.log(1e-3), math.log(1e-1)))
    inp['b_dt_bias'] = dt + jnp.log(-jnp.expm1(-dt))
    inp['b_norm_w'] = 1.0 + nrm((N_B, B_DV), 0.02)
    inp['w_b_out'] = nrm((N_B, B_WIDTH, D_MODEL), DN_BETA * B_WIDTH ** -0.5)
    inp['w_c_in'] = nrm((N_C, D_MODEL, c_cols), D_MODEL ** -0.5)
    inp['c_q_norm'] = 1.0 + nrm((N_C, C_Q_LORA), 0.02)
    inp['c_kv_norm'] = 1.0 + nrm((N_C, C_KV_LORA), 0.02)
    inp['w_c_uq'] = nrm((N_C, C_Q_LORA, C_HEADS * (C_NOPE + C_ROPE)), C_Q_LORA ** -0.5)
    inp['w_c_uk'] = nrm((N_C, C_KV_LORA, C_HEADS, C_NOPE), C_KV_LORA ** -0.5)
    inp['w_c_uv'] = nrm((N_C, C_KV_LORA, C_HEADS, C_V), C_KV_LORA ** -0.5)
    inp['w_c_out'] = nrm((N_C, C_HEADS * C_V, D_MODEL), DN_BETA * (C_HEADS * C_V) ** -0.5)
    inp['ln1_g'] = 1.0 + nrm((DEPTH, D_MODEL), 0.02)
    inp['ln1_b'] = nrm((DEPTH, D_MODEL), 0.02)
    inp['ln2_g'] = 1.0 + nrm((DEPTH, D_MODEL), 0.02)
    inp['ln2_b'] = nrm((DEPTH, D_MODEL), 0.02)
    inp['peer_wq'] = nrm((DEPTH, D_MODEL, P_HEADS * P_QDIM), D_MODEL ** -0.5)
    inp['peer_keys'] = nrm((DEPTH, P_HEADS, 2, N_KEYS, P_QDIM // 2), (P_QDIM // 2) ** -0.5)
    inp['peer_u'] = nrm((DEPTH, N_EXPERTS, D_MODEL), D_MODEL ** -0.5)
    inp['peer_v'] = nrm((DEPTH, N_EXPERTS, D_MODEL), DN_BETA * P_HEADS ** -0.5)
    return inp


def reference(x_prompt, x_sample, cache_a_win0, cache_a_win1, cache_a_win2, state_b_conv, state_b_rec,
              cache_c_latent, page_table, w_a_in, w_a_out, w_b_in, b_conv_w, b_a_log, b_dt_bias, b_norm_w,
              w_b_out, w_c_in, c_q_norm, c_kv_norm, w_c_uq, w_c_uk, w_c_uv, w_c_out, ln1_g, ln1_b, ln2_g,
              ln2_b, peer_wq, peer_keys, peer_u, peer_v):
    prm = dict(w_a_in=w_a_in, w_a_out=w_a_out, w_b_in=w_b_in, b_conv_w=b_conv_w, b_a_log=b_a_log,
               b_dt_bias=b_dt_bias, b_norm_w=b_norm_w, w_b_out=w_b_out, w_c_in=w_c_in, c_q_norm=c_q_norm,
               c_kv_norm=c_kv_norm, w_c_uq=w_c_uq, w_c_uk=w_c_uk, w_c_uv=w_c_uv, w_c_out=w_c_out,
               ln1_g=ln1_g, ln1_b=ln1_b, ln2_g=ln2_g, ln2_b=ln2_b, peer_wq=peer_wq, peer_keys=peer_keys,
               peer_u=peer_u, peer_v=peer_v)
    y_prompt, a_p, b_conv_p, b_rec_p, c_lat_p = run_trunk(
        x_prompt, jnp.arange(x_prompt.shape[1]), prm, None)
    past_len = page_table.shape[1] * cache_c_latent.shape[2]
    cache = dict(a_win=(cache_a_win0, cache_a_win1, cache_a_win2), b_conv=state_b_conv, b_rec=state_b_rec,
                 c_pool=cache_c_latent, page_table=page_table)
    y_sample, a_s, b_conv_s, b_rec_s, c_lat_s = run_trunk(
        x_sample, past_len + jnp.arange(x_sample.shape[1]), prm, cache)
    a_win0_p, a_win1_p, a_win2_p = a_p
    a_win0_s, a_win1_s, a_win2_s = a_s
    return (y_prompt, y_sample, a_win0_p, a_win1_p, a_win2_p, a_win0_s, a_win1_s, a_win2_s,
            b_conv_p, b_rec_p, b_conv_s, b_rec_s, c_lat_p, c_lat_s)
```

```python
import functools
import math

import numpy as np
import jax
import jax.numpy as jnp
from jax import lax
from jax.experimental import pallas as pl
from jax.experimental.pallas import tpu as pltpu

F32 = jnp.float32
BF16 = jnp.bfloat16
HI = lax.Precision.HIGHEST

DEPTH = 4
N_MIXERS = 3
DN_ALPHA = (2 * DEPTH) ** 0.25
LN_EPS = 1e-5
RMS_EPS = 1e-6

A_WINDOWS = (128, 512, 2048)
A_DILATIONS = (1, 4, 16)
A_GROUPS = 3
A_HEADS = 8
A_HEAD_DIM = 64
A_SPAN = 128
A_GW = 3 * A_HEADS * A_HEAD_DIM
A_HW = A_HEADS * A_HEAD_DIM

B_HEADS = 8
B_DK = 128
B_DV = 128
B_WIDTH = B_HEADS * B_DK
CONV_W = 4
B_CHUNK = 64

C_HEADS = 8
C_NOPE = 128
C_ROPE = 64
C_V = 128
C_KV_LORA = 256
C_Q_LORA = 384
C_LAT = C_KV_LORA + C_ROPE
C_SCALE = (C_NOPE + C_ROPE) ** -0.5
ROPE_THETA = 10000.0

P_HEADS = 8
N_KEYS = 128
P_QDIM = 256
P_TOPK = 16

LANES = 128
VMEM_LIMIT = 56 * 1024 * 1024
NEG_BIG = -1e30


def _cparams(*sem):
    return pltpu.CompilerParams(dimension_semantics=sem, vmem_limit_bytes=VMEM_LIMIT)


def _round_up(n, m):
    return -(-n // m) * m


def _row_tile(m):
    for t in (512, 256, 128, 64, 32, 16, 8):
        if m % t == 0:
            return t
    raise ValueError(m)


def _layer_norm(v, g, b):
    mu = jnp.mean(v, -1, keepdims=True)
    d = v - mu
    var = jnp.mean(d * d, -1, keepdims=True)
    return d * lax.rsqrt(var + LN_EPS) * g + b


def _bdot(a, b):
    return jnp.dot(a.astype(BF16), b.astype(BF16), preferred_element_type=F32)


def _bdot_nt(a, b):
    return lax.dot_general(a.astype(BF16), b.astype(BF16), (((1,), (1,)), ((), ())),
                           preferred_element_type=F32)


def _fdot(a, b):
    return jnp.dot(a, b, precision=HI, preferred_element_type=F32)


def _softplus(v):
    return jnp.maximum(v, 0.0) + jnp.log1p(jnp.exp(-jnp.abs(v)))


def _mm_kernel(x_ref, w_ref, o_ref):
    o_ref[...] = _bdot(x_ref[...], w_ref[...])


def matmul(x, w):
    m, k = x.shape
    n = w.shape[1]
    tm = min(_row_tile(m), 256)
    return pl.pallas_call(
        _mm_kernel,
        grid=(m // tm,),
        in_specs=[pl.BlockSpec((tm, k), lambda i: (i, 0)),
                  pl.BlockSpec((k, n), lambda i: (0, 0))],
        out_specs=pl.BlockSpec((tm, n), lambda i: (i, 0)),
        out_shape=jax.ShapeDtypeStruct((m, n), F32),
        compiler_params=_cparams("parallel"),
        name="matmul",
    )(x, w)


def _mm_res_ln_kernel(h_ref, w_ref, x_ref, g_ref, b_ref, o_ref):
    y = _bdot(h_ref[...], w_ref[...])
    o_ref[...] = _layer_norm(DN_ALPHA * x_ref[...] + y, g_ref[...], b_ref[...])


def matmul_res_ln(h, w, x, g, b):
    m, k = h.shape
    n = w.shape[1]
    tm = min(_row_tile(m), 256)
    return pl.pallas_call(
        _mm_res_ln_kernel,
        grid=(m // tm,),
        in_specs=[pl.BlockSpec((tm, k), lambda i: (i, 0)),
                  pl.BlockSpec((k, n), lambda i: (0, 0)),
                  pl.BlockSpec((tm, n), lambda i: (i, 0)),
                  pl.BlockSpec((1, n), lambda i: (0, 0)),
                  pl.BlockSpec((1, n), lambda i: (0, 0))],
        out_specs=pl.BlockSpec((tm, n), lambda i: (i, 0)),
        out_shape=jax.ShapeDtypeStruct((m, n), F32),
        compiler_params=_cparams("parallel"),
        name="matmul_res_ln",
    )(h, w, x, g.reshape(1, n), b.reshape(1, n))


def _attn_block(q_blk, kv_blk, valid, o_ref, l_ref, row0):
    lane = lax.broadcasted_iota(jnp.int32, (A_SPAN, LANES), 1)
    low = lane < A_HEAD_DIM
    for pair in range(A_HEADS // 2):
        c0 = pair * LANES
        qp = q_blk[:, c0:c0 + LANES]
        kp = kv_blk[:, A_HW + c0:A_HW + c0 + LANES].astype(BF16)
        vp = kv_blk[:, 2 * A_HW + c0:2 * A_HW + c0 + LANES].astype(BF16)
        outs, lses = [], []
        for sel in (low, jnp.logical_not(low)):
            qh = jnp.where(sel, qp, 0.0)
            s = _bdot_nt(qh, kp) * (A_HEAD_DIM ** -0.5)
            s = jnp.where(valid, s, -jnp.inf)
            mx = jnp.max(s, -1, keepdims=True)
            e = jnp.exp(s - mx)
            den = jnp.sum(e, -1, keepdims=True)
            p = e * (1.0 / den)
            outs.append(_bdot(p, vp))
            lses.append(mx + jnp.log(den))
        o_ref[0, pl.ds(row0, A_SPAN), c0:c0 + LANES] = jnp.where(low, outs[0], outs[1])
        l_ref[0, pl.ds(row0, A_SPAN), c0:c0 + LANES] = jnp.where(
            low, jnp.broadcast_to(lses[0], (A_SPAN, LANES)), jnp.broadcast_to(lses[1], (A_SPAN, LANES)))


def _attn_prompt_kernel(cur_ref, halo_ref, o_ref, l_ref, *, tl, use_halo):
    ti = pl.program_id(2)
    qi = lax.broadcasted_iota(jnp.int32, (A_SPAN, 2 * A_SPAN), 0) + A_SPAN
    ki = lax.broadcasted_iota(jnp.int32, (A_SPAN, 2 * A_SPAN), 1)
    band = (qi - ki >= 0) & (qi - ki <= A_SPAN)
    causal = band[:, A_SPAN:]
    for j in range(tl // A_SPAN):
        row0 = j * A_SPAN
        q_blk = cur_ref[0, pl.ds(row0, A_SPAN), :]
        if j == 0 and not use_halo:
            _attn_block(q_blk, q_blk, causal, o_ref, l_ref, row0)
            continue
        if j == 0:
            kv_blk = jnp.concatenate([halo_ref[0], q_blk], axis=0)
            valid = band & ((ki >= A_SPAN) | (ti > 0))
        else:
            kv_blk = cur_ref[0, pl.ds(row0 - A_SPAN, 2 * A_SPAN), :]
            valid = band
        _attn_block(q_blk, kv_blk, valid, o_ref, l_ref, row0)


def attn_prompt(proj, g):
    b, t, w = proj.shape
    dil = A_DILATIONS[g]
    sub = t // dil
    tl = min(sub, 512)
    nt = sub // tl
    use_halo = sub > A_SPAN
    pv = proj.reshape(b, sub, dil * w)
    per = tl // A_SPAN
    kern = functools.partial(_attn_prompt_kernel, tl=tl, use_halo=use_halo)
    o, l = pl.pallas_call(
        kern,
        grid=(b, dil, nt),
        in_specs=[pl.BlockSpec((1, tl, A_GW), lambda bi, r, ti: (bi, ti, r * A_GROUPS + g)),
                  pl.BlockSpec((1, A_SPAN, A_GW),
                               lambda bi, r, ti: (bi, jnp.maximum(ti * per - 1, 0), r * A_GROUPS + g))],
        out_specs=[pl.BlockSpec((1, tl, A_HW), lambda bi, r, ti: (bi, ti, r)),
                   pl.BlockSpec((1, tl, A_HW), lambda bi, r, ti: (bi, ti, r))],
        out_shape=[jax.ShapeDtypeStruct((b, sub, dil * A_HW), F32)] * 2,
        compiler_params=_cparams("parallel", "parallel", "parallel"),
        name="attn_prompt_g%d" % g,
    )(pv, pv)
    return o.reshape(b * t, A_HW), l.reshape(b * t, A_HW)


def _seg_matrices():
    d = np.arange(A_HW)[:, None] // A_HEAD_DIM
    c = np.arange(LANES)[None, :]
    seg = (d == c).astype(np.float32)
    return jnp.asarray(seg), jnp.asarray(seg.T)


def _attn_step_kernel(proj_ref, c0_ref, c1_ref, c2_ref, seg_ref, segt_ref, o_ref, l_ref, *, ts):
    seg = seg_ref[...]
    segt = segt_ref[...]
    nrow = A_SPAN + 8
    rows = lax.broadcasted_iota(jnp.int32, (nrow, LANES), 0)
    caches = (c0_ref, c1_ref, c2_ref)
    kv_w = 2 * A_HW
    for g in range(A_GROUPS):
        dil = A_DILATIONS[g]
        base = g * A_GW
        knew = proj_ref[0, :, base + A_HW:base + 2 * A_HW]
        vnew = proj_ref[0, :, base + 2 * A_HW:base + 3 * A_HW]
        for t in range(ts):
            q = proj_ref[0, t:t + 1, base:base + A_HW]
            res = 0 if dil == 1 else t
            kfull = jnp.concatenate([caches[g][0, :, res * kv_w:res * kv_w + A_HW], knew], axis=0)
            vfull = jnp.concatenate([caches[g][0, :, res * kv_w + A_HW:(res + 1) * kv_w], vnew], axis=0)
            s = _fdot(kfull * q, seg) * (A_HEAD_DIM ** -0.5)
            if dil == 1:
                valid = ((rows < A_SPAN) & (rows >= t)) | ((rows >= A_SPAN) & (rows <= A_SPAN + t))
            else:
                valid = (rows < A_SPAN) | (rows == A_SPAN + t)
            s = jnp.where(valid, s, -jnp.inf)
            mx = jnp.max(s, 0, keepdims=True)
            e = jnp.exp(s - mx)
            den = jnp.sum(e, 0, keepdims=True)
            p = e * (1.0 / den)
            o = jnp.sum(_fdot(p, segt) * vfull, 0, keepdims=True)
            lse = jnp.broadcast_to(mx + jnp.log(den), (8, LANES))
            o_ref[0, t:t + 1, g * A_HW:(g + 1) * A_HW] = o
            l_ref[0, t:t + 1, g * A_HW:(g + 1) * A_HW] = _fdot(lse, segt)[0:1]


def attn_step(proj, caches, ts):
    b, tp, w = proj.shape
    seg, segt = _seg_matrices()
    views, specs = [], []
    for g in range(A_GROUPS):
        dil = A_DILATIONS[g]
        assert caches[g].shape[1] == A_SPAN * dil and ts <= dil or dil == 1
        views.append(caches[g].reshape(b, A_SPAN, dil * 2 * A_HW))
        nres = 1 if dil == 1 else ts
        specs.append(pl.BlockSpec((1, A_SPAN, nres * 2 * A_HW), lambda bi: (bi, 0, 0)))
    kern = functools.partial(_attn_step_kernel, ts=ts)
    return pl.pallas_call(
        kern,
        grid=(b,),
        in_specs=[pl.BlockSpec((1, tp, w), lambda bi: (bi, 0, 0))] + specs + [
            pl.BlockSpec((A_HW, LANES), lambda bi: (0, 0)),
            pl.BlockSpec((LANES, A_HW), lambda bi: (0, 0))],
        out_specs=[pl.BlockSpec((1, ts, A_GROUPS * A_HW), lambda bi: (bi, 0, 0))] * 2,
        out_shape=[jax.ShapeDtypeStruct((b, ts, A_GROUPS * A_HW), F32)] * 2,
        compiler_params=_cparams("parallel"),
        name="attn_step",
    )(proj, *views, seg, segt)


def _a_out_kernel(o0, o1, o2, l0, l1, l2, w_ref, x_ref, g_ref, b_ref, out_ref):
    ls = (l0[...], l1[...], l2[...])
    mx = jnp.maximum(jnp.maximum(ls[0], ls[1]), ls[2])
    es = [jnp.exp(l - mx) for l in ls]
    inv = 1.0 / (es[0] + es[1] + es[2])
    o = (es[0] * inv) * o0[...] + (es[1] * inv) * o1[...] + (es[2] * inv) * o2[...]
    y = _bdot(o, w_ref[...])
    out_ref[...] = _layer_norm(DN_ALPHA * x_ref[...] + y, g_ref[...], b_ref[...])


def a_out(os_, ls_, w, x, g, b):
    m, d = x.shape
    tm = min(_row_tile(m), 256)
    if isinstance(os_, (list, tuple)):
        ol = list(os_) + list(ls_)
        ospecs = [pl.BlockSpec((tm, A_HW), lambda i: (i, 0))] * 6
    else:
        ol = [os_] * 3 + [ls_] * 3
        ospecs = [pl.BlockSpec((tm, A_HW), functools.partial(lambda i, gg: (i, gg), gg=gg))
                  for gg in range(3)] * 2
    return pl.pallas_call(
        _a_out_kernel,
        grid=(m // tm,),
        in_specs=ospecs + [pl.BlockSpec((A_HW, d), lambda i: (0, 0)),
                           pl.BlockSpec((tm, d), lambda i: (i, 0)),
                           pl.BlockSpec((1, d), lambda i: (0, 0)),
                           pl.BlockSpec((1, d), lambda i: (0, 0))],
        out_specs=pl.BlockSpec((tm, d), lambda i: (i, 0)),
        out_shape=jax.ShapeDtypeStruct((m, d), F32),
        compiler_params=_cparams("parallel"),
        name="a_out",
    )(*ol, w, x, g.reshape(1, d), b.reshape(1, d))


def _split_bf16(a):
    hi = a.astype(BF16)
    lo = (a - hi.astype(F32)).astype(BF16)
    return hi, lo


def _dot3(a, b):
    ah, al = _split_bf16(a)
    bh, bl = _split_bf16(b)
    d = functools.partial(jnp.dot, preferred_element_type=F32)
    return d(ah, bh) + (d(ah, bl) + d(al, bh))


def _gdn_kernel(proj_ref, abt_ref, cbuf_ref, s0_ref, convw_ref, alog_row_ref, dt_row_ref, alog_col_ref,
                dt_col_ref, normw_ref, o_ref, s_ref, xp_ref, *, c, t_valid):
    ci = pl.program_id(1)
    qkv_w = 3 * B_WIDTH

    @pl.when(ci == 0)
    def _():
        s_ref[...] = s0_ref[...]
        xp_ref[8 - (CONV_W - 1):8, :] = cbuf_ref[0]

    xp_ref[8:8 + c, :] = proj_ref[0, :, 0:qkv_w]
    y = None
    for j in range(CONV_W):
        lo = 8 - (CONV_W - 1) + j
        term = xp_ref[lo:lo + c, :] * convw_ref[j:j + 1, :]
        y = term if y is None else y + term
    xp_ref[8 - (CONV_W - 1):8, :] = xp_ref[8 + c - (CONV_W - 1):8 + c, :]
    qkv = y * jax.nn.sigmoid(y)

    ab = proj_ref[0, :, 4 * B_WIDTH:4 * B_WIDTH + LANES]
    trow = lax.broadcasted_iota(jnp.int32, (c, LANES), 0) + ci * c
    live = trow < t_valid
    beta = jnp.where(live, jax.nn.sigmoid(ab), 0.0)
    gate = jnp.where(live, -jnp.exp(alog_row_ref[...]) * _softplus(ab + dt_row_ref[...]), 0.0)
    abt = abt_ref[0, 0]
    tcol = lax.broadcasted_iota(jnp.int32, (B_HEADS, c), 1) + ci * c
    gate_t = jnp.where(tcol < t_valid,
                       -jnp.exp(alog_col_ref[...]) * _softplus(abt[B_HEADS:2 * B_HEADS] + dt_col_ref[...]), 0.0)
    ii = lax.broadcasted_iota(jnp.int32, (c, c), 0)
    jj = lax.broadcasted_iota(jnp.int32, (c, c), 1)
    tri = ii >= jj
    stri = ii > jj
    gcum = _fdot(tri.astype(F32), gate)
    gcum_t = _fdot(gate_t, (ii <= jj).astype(F32))
    eye = (ii == jj).astype(F32)

    for h in range(B_HEADS):
        sl = slice(h * B_DK, (h + 1) * B_DK)
        qh = qkv[:, sl]
        kh = qkv[:, B_WIDTH + h * B_DK:B_WIDTH + (h + 1) * B_DK]
        vh = qkv[:, 2 * B_WIDTH + h * B_DV:2 * B_WIDTH + (h + 1) * B_DV]
        qh = qh * lax.rsqrt(jnp.sum(qh * qh, -1, keepdims=True) + 1e-6) * (B_DK ** -0.5)
        kh = kh * lax.rsqrt(jnp.sum(kh * kh, -1, keepdims=True) + 1e-6)
        bh = beta[:, h:h + 1]
        gc = gcum[:, B_HEADS + h:B_HEADS + h + 1]
        gr = gcum_t[h:h + 1, :]
        glast = gcum[c - 1:c, B_HEADS + h:B_HEADS + h + 1]
        dmat = jnp.exp(jnp.where(tri, gc - gr, -jnp.inf))
        kb = kh * bh
        vb = vh * bh
        kq = _bdot_nt(jnp.concatenate([kb, qh], axis=0), kh)
        a = jnp.where(stri, kq[0:c] * dmat, 0.0)
        qk = jnp.where(tri, kq[c:2 * c] * dmat, 0.0)
        mpow = -a
        tinv = eye + mpow
        for _ in range(int(math.log2(c)) - 1):
            mpow = _dot3(mpow, mpow)
            tinv = tinv + _dot3(tinv, mpow)
        egc = jnp.exp(gc)
        wu = _bdot(tinv, jnp.concatenate([kb * egc, vb], axis=1))
        s_h = s_ref[0, h]
        ws = _bdot(jnp.concatenate([wu[:, 0:B_DK], qh * egc], axis=0), s_h)
        vnew = wu[:, B_DK:B_DK + B_DV] - ws[0:c]
        oh = ws[c:2 * c] + _bdot(qk, vnew)
        kdec = kh * jnp.exp(glast - gc)
        upd = lax.dot_general(kdec.astype(BF16), vnew.astype(BF16), (((0,), (0,)), ((), ())),
                              preferred_element_type=F32)
        s_ref[0, h] = s_h * jnp.exp(glast) + upd
        zh = proj_ref[0, :, qkv_w + h * B_DV:qkv_w + (h + 1) * B_DV]
        on = oh * lax.rsqrt(jnp.mean(oh * oh, -1, keepdims=True) + RMS_EPS) * normw_ref[...]
        o_ref[0, :, h * B_DV:(h + 1) * B_DV] = on * (zh * jax.nn.sigmoid(zh))


def gdn(proj, conv_buf, s0, conv_w, a_log, dt_bias, norm_w, t_valid):
    b, t, w = proj.shape
    c = min(B_CHUNK, t)
    n = t // c
    ab = proj[:, :, 4 * B_WIDTH:4 * B_WIDTH + 2 * B_HEADS]
    abt = ab.reshape(b, n, c, 2 * B_HEADS).transpose(0, 1, 3, 2)
    lane_pad = LANES - 2 * B_HEADS
    alog_row = jnp.pad(a_log.reshape(1, B_HEADS), ((0, 0), (B_HEADS, lane_pad)))
    dt_row = jnp.pad(dt_bias.reshape(1, B_HEADS), ((0, 0), (B_HEADS, lane_pad)))
    kern = functools.partial(_gdn_kernel, c=c, t_valid=t_valid)
    const2 = lambda bi, ci: (0, 0)
    return pl.pallas_call(
        kern,
        grid=(b, n),
        in_specs=[pl.BlockSpec((1, c, w), lambda bi, ci: (bi, ci, 0)),
                  pl.BlockSpec((1, 1, 2 * B_HEADS, c), lambda bi, ci: (bi, ci, 0, 0)),
                  pl.BlockSpec((1, CONV_W - 1, 3 * B_WIDTH), lambda bi, ci: (bi, 0, 0)),
                  pl.BlockSpec((1, B_HEADS, B_DK, B_DV), lambda bi, ci: (bi, 0, 0, 0)),
                  pl.BlockSpec((CONV_W, 3 * B_WIDTH), const2),
                  pl.BlockSpec((1, LANES), const2),
                  pl.BlockSpec((1, LANES), const2),
                  pl.BlockSpec((B_HEADS, 1), const2),
                  pl.BlockSpec((B_HEADS, 1), const2),
                  pl.BlockSpec((1, B_DV), const2)],
        out_specs=[pl.BlockSpec((1, c, B_WIDTH), lambda bi, ci: (bi, ci, 0)),
                   pl.BlockSpec((1, B_HEADS, B_DK, B_DV), lambda bi, ci: (bi, 0, 0, 0))],
        out_shape=[jax.ShapeDtypeStruct((b, t, B_WIDTH), F32),
                   jax.ShapeDtypeStruct((b, B_HEADS, B_DK, B_DV), F32)],
        scratch_shapes=[pltpu.VMEM((8 + c, 3 * B_WIDTH), F32)],
        compiler_params=_cparams("parallel", "arbitrary"),
        name="gdn",
    )(proj, abt, conv_buf, s0, conv_w, alog_row, dt_row, a_log.reshape(B_HEADS, 1),
      dt_bias.reshape(B_HEADS, 1), norm_w.reshape(1, B_DV))


def _rot_half(x, width):
    lane = lax.broadcasted_iota(jnp.int32, x.shape, 1)
    first = (lane % C_ROPE) < C_ROPE // 2
    return jnp.where(first, -pltpu.roll(x, width - C_ROPE // 2, 1), pltpu.roll(x, C_ROPE // 2, 1))


def _mla_prep_kernel(proj_ref, cs_ref, qn_ref, kvn_ref, wuq_ref, wuk_ref, lat_ref, qcat_ref):
    rw = C_HEADS * C_ROPE
    cos_all = cs_ref[:, 0:rw]
    sin_all = cs_ref[:, rw:2 * rw]
    pc = proj_ref[...]
    cq = pc[:, 0:C_Q_LORA]
    cq = cq * lax.rsqrt(jnp.mean(cq * cq, -1, keepdims=True) + RMS_EPS) * qn_ref[...]
    ckv = pc[:, C_Q_LORA:C_Q_LORA + C_KV_LORA]
    ckv = ckv * lax.rsqrt(jnp.mean(ckv * ckv, -1, keepdims=True) + RMS_EPS) * kvn_ref[...]
    lat_ref[:, 0:C_KV_LORA] = ckv
    kr = pc[:, C_Q_LORA + C_KV_LORA:C_Q_LORA + C_KV_LORA + LANES]
    kr = kr * cos_all[:, 0:LANES] + _rot_half(kr, LANES) * sin_all[:, 0:LANES]
    lat_ref[:, C_KV_LORA:C_LAT] = kr[:, 0:C_ROPE]
    q = _bdot(cq, wuq_ref[...])
    nw = C_HEADS * C_NOPE
    qr = q[:, nw:nw + rw]
    qr = qr * cos_all + _rot_half(qr, rw) * sin_all
    for h in range(C_HEADS):
        qcat_ref[h, :, 0:C_KV_LORA] = _bdot(q[:, h * C_NOPE:(h + 1) * C_NOPE], wuk_ref[h])
        qcat_ref[h, :, C_KV_LORA:C_LAT] = qr[:, h * C_ROPE:(h + 1) * C_ROPE]


def mla_prep(proj, cs, q_norm, kv_norm, w_uq_perm, w_uk_t):
    m, w = proj.shape
    tm = min(_row_tile(m), 256)
    nrep = cs.shape[0] // tm
    const2 = lambda i: (0, 0)
    return pl.pallas_call(
        _mla_prep_kernel,
        grid=(m // tm,),
        in_specs=[pl.BlockSpec((tm, w), lambda i: (i, 0)),
                  pl.BlockSpec((tm, cs.shape[1]), lambda i: (i % nrep, 0)),
                  pl.BlockSpec((1, C_Q_LORA), const2),
                  pl.BlockSpec((1, C_KV_LORA), const2),
                  pl.BlockSpec(w_uq_perm.shape, const2),
                  pl.BlockSpec(w_uk_t.shape, lambda i: (0, 0, 0))],
        out_specs=[pl.BlockSpec((tm, C_LAT), lambda i: (i, 0)),
                   pl.BlockSpec((C_HEADS, tm, C_LAT), lambda i: (0, i, 0))],
        out_shape=[jax.ShapeDtypeStruct((m, C_LAT), F32),
                   jax.ShapeDtypeStruct((C_HEADS, m, C_LAT), F32)],
        compiler_params=_cparams("parallel"),
        name="mla_prep",
    )(proj, cs, q_norm.reshape(1, -1), kv_norm.reshape(1, -1), w_uq_perm, w_uk_t)


def _mla_flash_kernel(q_ref, lat_ref, wuv_ref, o_ref, m_ref, l_ref, acc_ref, *, tq, tk):
    qi = pl.program_id(1)
    ki = pl.program_id(2)
    last = (qi * tq + tq - 1) // tk
    rows = C_HEADS * tq

    @pl.when(ki == 0)
    def _():
        m_ref[...] = jnp.full_like(m_ref, -jnp.inf)
        l_ref[...] = jnp.zeros_like(l_ref)
        acc_ref[...] = jnp.zeros_like(acc_ref)

    @pl.when(ki <= last)
    def _():
        q = q_ref[...].reshape(rows, C_LAT)
        lat = lat_ref[0]
        s = _bdot_nt(q, lat) * C_SCALE
        qpos = qi * tq + lax.broadcasted_iota(jnp.int32, (tq, tk), 0)
        kpos = ki * tk + lax.broadcasted_iota(jnp.int32, (tq, tk), 1)
        s = jnp.where((kpos <= qpos)[None], s.reshape(C_HEADS, tq, tk), -jnp.inf).reshape(rows, tk)
        m_old = m_ref[...]
        m_new = jnp.maximum(m_old, jnp.max(s, -1, keepdims=True))
        alpha = jnp.exp(m_old - m_new)
        p = jnp.exp(s - m_new)
        l_ref[...] = alpha * l_ref[...] + jnp.sum(p, -1, keepdims=True)
        acc_ref[...] = alpha * acc_ref[...] + _bdot(p, lat[:, 0:C_KV_LORA])
        m_ref[...] = m_new

    @pl.when(ki == last)
    def _():
        o = acc_ref[...] * (1.0 / l_ref[...])
        for h in range(C_HEADS):
            o_ref[:, h * C_V:(h + 1) * C_V] = _bdot(o[h * tq:(h + 1) * tq], wuv_ref[h])


def mla_flash(qcat, lat, w_uv_t, b, t):
    tq, tk = 128, 512
    tk = min(tk, t)
    tq = min(tq, t)
    nq, nk = t // tq, t // tk
    kern = functools.partial(_mla_flash_kernel, tq=tq, tk=tk)
    rows = C_HEADS * tq
    return pl.pallas_call(
        kern,
        grid=(b, nq, nk),
        in_specs=[pl.BlockSpec((C_HEADS, tq, C_LAT), lambda bi, qi, ki: (0, bi * nq + qi, 0)),
                  pl.BlockSpec((1, tk, C_LAT),
                               lambda bi, qi, ki: (bi, jnp.minimum(ki, (qi * tq + tq - 1) // tk), 0)),
                  pl.BlockSpec(w_uv_t.shape, lambda bi, qi, ki: (0, 0, 0))],
        out_specs=pl.BlockSpec((tq, C_HEADS * C_V), lambda bi, qi, ki: (bi * nq + qi, 0)),
        out_shape=jax.ShapeDtypeStruct((b * t, C_HEADS * C_V), F32),
        scratch_shapes=[pltpu.VMEM((rows, 1), F32), pltpu.VMEM((rows, 1), F32),
                        pltpu.VMEM((rows, C_KV_LORA), F32)],
        compiler_params=_cparams("parallel", "parallel", "arbitrary"),
        name="mla_flash",
    )(qcat, lat, w_uv_t)


def _mla_step_kernel(pt_ref, q_ref, new_ref, *rest, ts, npg):
    page_refs = rest[:npg]
    wuv_ref, o_ref, m_ref, l_ref, acc_ref = rest[npg:]
    si = pl.program_id(1)
    rows = ts * C_HEADS
    q = q_ref[0]

    @pl.when(si == 0)
    def _():
        new = new_ref[0]
        tq_ = lax.broadcasted_iota(jnp.int32, (rows, 1), 0) // C_HEADS
        s_new = []
        for tn in range(ts):
            sn = jnp.sum(q * new[tn:tn + 1, :], -1, keepdims=True) * C_SCALE
            s_new.append(jnp.where(tq_ >= tn, sn, -jnp.inf))
        mx = s_new[0]
        for sn in s_new[1:]:
            mx = jnp.maximum(mx, sn)
        den = jnp.zeros((rows, 1), F32)
        acc = jnp.zeros((rows, C_KV_LORA), F32)
        for tn in range(ts):
            e = jnp.exp(s_new[tn] - mx)
            den = den + e
            acc = acc + e * new[tn:tn + 1, 0:C_KV_LORA]
        m_ref[...] = mx
        l_ref[...] = den
        acc_ref[...] = acc

    s = jnp.concatenate([_bdot_nt(q, pr[0]) for pr in page_refs], axis=1) * C_SCALE
    m_old = m_ref[...]
    m_new = jnp.maximum(m_old, jnp.max(s, -1, keepdims=True))
    alpha = jnp.exp(m_old - m_new)
    p = jnp.exp(s - m_new)
    l_ref[...] = alpha * l_ref[...] + jnp.sum(p, -1, keepdims=True)
    acc = alpha * acc_ref[...]
    psz = page_refs[0].shape[1]
    for i, pr in enumerate(page_refs):
        acc = acc + _bdot(p[:, i * psz:(i + 1) * psz], pr[0, :, 0:C_KV_LORA])
    acc_ref[...] = acc
    m_ref[...] = m_new

    @pl.when(si == pl.num_programs(1) - 1)
    def _():
        o = acc_ref[...] * (1.0 / l_ref[...])
        full = _bdot(o, wuv_ref[...])
        hrow = lax.broadcasted_iota(jnp.int32, (C_HEADS, C_HEADS * C_V), 0)
        hcol = lax.broadcasted_iota(jnp.int32, (C_HEADS, C_HEADS * C_V), 1) // C_V
        diag = hrow == hcol
        for t in range(ts):
            blk = full[t * C_HEADS:(t + 1) * C_HEADS]
            o_ref[0, t:t + 1, :] = jnp.sum(jnp.where(diag, blk, 0.0), 0, keepdims=True)


def mla_step(qrows, lat_new, pool, page_table, w_uv_flat):
    b, rows, _ = qrows.shape
    ts = rows // C_HEADS
    n_pages = page_table.shape[1]
    psz = pool.shape[1]
    npg = 16
    while n_pages % npg:
        npg //= 2
    kern = functools.partial(_mla_step_kernel, ts=ts, npg=npg)

    def page_map(bi, si, pt, k):
        return (pt[bi, si * npg + k], 0, 0)

    grid_spec = pltpu.PrefetchScalarGridSpec(
        num_scalar_prefetch=1,
        grid=(b, n_pages // npg),
        in_specs=[pl.BlockSpec((1, rows, C_LAT), lambda bi, si, pt: (bi, 0, 0)),
                  pl.BlockSpec((1, ts, C_LAT), lambda bi, si, pt: (bi, 0, 0))]
                 + [pl.BlockSpec((1, psz, C_LAT), functools.partial(page_map, k=k)) for k in range(npg)]
                 + [pl.BlockSpec(w_uv_flat.shape, lambda bi, si, pt: (0, 0))],
        out_specs=pl.BlockSpec((1, ts, C_HEADS * C_V), lambda bi, si, pt: (bi, 0, 0)),
        scratch_shapes=[pltpu.VMEM((rows, 1), F32), pltpu.VMEM((rows, 1), F32),
                        pltpu.VMEM((rows, C_KV_LORA), F32)],
    )
    return pl.pallas_call(
        kern,
        grid_spec=grid_spec,
        out_shape=jax.ShapeDtypeStruct((b, ts, C_HEADS * C_V), F32),
        compiler_params=_cparams("parallel", "arbitrary"),
        name="mla_step",
    )(page_table, qrows, lat_new, *([pool] * npg), w_uv_flat)


def _pruned_pairs():
    return [(i, j) for i in range(P_TOPK) for j in range(P_TOPK // (i + 1))]


def _extract_top(s, n):
    vals = []
    for _ in range(n):
        mx = jnp.max(s, 0, keepdims=True)
        vals.append(mx)
        s = jnp.where(s == mx, -jnp.inf, s)
    return vals


def _candidates(avals, bvals, tn):
    pairs = _pruned_pairs()
    sub = lax.broadcasted_iota(jnp.int32, (8, tn), 0)
    groups = []
    for g0 in range(0, len(pairs), 8):
        grp = pairs[g0:g0 + 8]
        av = jnp.broadcast_to(avals[grp[0][0]], (8, tn))
        bv = jnp.broadcast_to(bvals[grp[0][1]], (8, tn))
        for r in range(1, len(grp)):
            if grp[r][0] != grp[r - 1][0]:
                av = jnp.where(sub >= r, avals[grp[r][0]], av)
            if grp[r][1] != grp[r - 1][1]:
                bv = jnp.where(sub >= r, bvals[grp[r][1]], bv)
        cand = av + bv
        if len(grp) < 8:
            cand = jnp.where(sub < len(grp), cand, -jnp.inf)
        groups.append(cand)
    return jnp.concatenate(groups, axis=0)


def _peer_topk_kernel(q_ref, keys_ref, s1_ref, s2_ref, thr_ref):
    tn = q_ref.shape[0]
    half = P_QDIM // 2
    sub = lax.broadcasted_iota(jnp.int32, (P_HEADS, tn), 0)
    thr = jnp.zeros((P_HEADS, tn), F32)
    for h in range(P_HEADS):
        sc = []
        for c in range(2):
            qs = q_ref[:, h * P_QDIM + c * half:h * P_QDIM + (c + 1) * half]
            s = _bdot_nt(keys_ref[h, c], qs)
            sc.append(s - jnp.max(s, 0, keepdims=True))
        a = _extract_top(sc[0], P_TOPK)
        bvals = _extract_top(sc[1], P_TOPK)
        top = _extract_top(_candidates(a, bvals, tn), P_TOPK)
        z = jnp.exp(top[0])
        for v in top[1:]:
            z = z + jnp.exp(v)
        logz = jnp.log(z)
        top2 = _extract_top(_candidates([v - logz for v in a], bvals, tn), P_TOPK)
        s1_ref[h] = sc[0] - logz
        s2_ref[h] = sc[1]
        thr = jnp.where(sub == h, top2[P_TOPK - 1], thr)
    thr_ref[...] = thr


def peer_topk(q, keys_bf16):
    m = q.shape[0]
    tn = LANES
    return pl.pallas_call(
        _peer_topk_kernel,
        grid=(m // tn,),
        in_specs=[pl.BlockSpec((tn, q.shape[1]), lambda i: (i, 0)),
                  pl.BlockSpec(keys_bf16.shape, lambda i: (0, 0, 0, 0))],
        out_specs=[pl.BlockSpec((P_HEADS, N_KEYS, tn), lambda i: (0, 0, i)),
                   pl.BlockSpec((P_HEADS, N_KEYS, tn), lambda i: (0, 0, i)),
                   pl.BlockSpec((P_HEADS, tn), lambda i: (0, i))],
        out_shape=[jax.ShapeDtypeStruct((P_HEADS, N_KEYS, m), F32),
                   jax.ShapeDtypeStruct((P_HEADS, N_KEYS, m), F32),
                   jax.ShapeDtypeStruct((P_HEADS, m), F32)],
        compiler_params=_cparams("parallel"),
        name="peer_topk",
    )(q, keys_bf16)


def _gelu(v):
    return 0.5 * v * (1.0 + lax.erf(v * (2.0 ** -0.5)))


def _peer_dense_kernel(x_ref, s1_ref, s2_ref, thr_ref, u_ref, vt_ref, g_ref, b_ref, o_ref,
                       xb_ref, ht_ref, at_ref, acc_ref, *, n1):
    ei = pl.program_id(1)
    tm = x_ref.shape[0]
    nc = tm // LANES

    @pl.when(ei == 0)
    def _():
        xb_ref[...] = x_ref[...].astype(BF16)
        acc_ref[...] = jnp.zeros_like(acc_ref)

    ht_ref[...] = lax.dot_general(u_ref[...], xb_ref[...], (((1,), (1,)), ((), ())),
                                  preferred_element_type=F32)

    i1_base = pl.multiple_of(ei * n1, n1)

    def body(ci, carry):
        col = pl.multiple_of(ci * LANES, LANES)
        s1_rows = [s1_ref[h, pl.ds(i1_base, n1), pl.ds(col, LANES)] for h in range(P_HEADS)]
        thr = thr_ref[:, pl.ds(col, LANES)]
        for j in range(n1):
            gsum = jnp.zeros((N_KEYS, LANES), F32)
            for h in range(P_HEADS):
                ssum = s1_rows[h][j:j + 1, :] + s2_ref[h, :, pl.ds(col, LANES)]
                keep = ssum >= thr[h:h + 1, :]
                gsum = gsum + jnp.exp(jnp.where(keep, ssum, NEG_BIG))
            rows = slice(j * N_KEYS, (j + 1) * N_KEYS)
            act = _gelu(ht_ref[rows, pl.ds(col, LANES)]) * gsum
            at_ref[rows, pl.ds(col, LANES)] = act.astype(BF16)
        return carry

    lax.fori_loop(0, nc, body, 0)
    acc_ref[...] += jnp.dot(vt_ref[...], at_ref[...], preferred_element_type=F32)

    @pl.when(ei == pl.num_programs(1) - 1)
    def _():
        o_ref[...] = _layer_norm(DN_ALPHA * x_ref[...] + acc_ref[...].T, g_ref[...], b_ref[...])


def peer_dense(x, s1, s2, thr, u_bf16, vt_bf16, g, b):
    m, d = x.shape
    ne = u_bf16.shape[0]
    tm = min(_row_tile(m), 512)
    n1 = 8
    te = n1 * N_KEYS
    kern = functools.partial(_peer_dense_kernel, n1=n1)
    return pl.pallas_call(
        kern,
        grid=(m // tm, ne // te),
        in_specs=[pl.BlockSpec((tm, d), lambda i, e: (i, 0)),
                  pl.BlockSpec((P_HEADS, N_KEYS, tm), lambda i, e: (0, 0, i)),
                  pl.BlockSpec((P_HEADS, N_KEYS, tm), lambda i, e: (0, 0, i)),
                  pl.BlockSpec((P_HEADS, tm), lambda i, e: (0, i)),
                  pl.BlockSpec((te, d), lambda i, e: (e, 0)),
                  pl.BlockSpec((d, te), lambda i, e: (0, e)),
                  pl.BlockSpec((1, d), lambda i, e: (0, 0)),
                  pl.BlockSpec((1, d), lambda i, e: (0, 0))],
        out_specs=pl.BlockSpec((tm, d), lambda i, e: (i, 0)),
        out_shape=jax.ShapeDtypeStruct((m, d), F32),
        scratch_shapes=[pltpu.VMEM((tm, d), BF16), pltpu.VMEM((te, tm), F32),
                        pltpu.VMEM((te, tm), BF16), pltpu.VMEM((d, tm), F32)],
        compiler_params=_cparams("parallel", "arbitrary"),
        name="peer_dense",
    )(x, s1, s2, thr, u_bf16, vt_bf16, g.reshape(1, d), b.reshape(1, d))


def peer_layer(x, wq_bf16, keys_bf16, u_bf16, vt_bf16, g, b):
    q = matmul(x, wq_bf16)
    s1, s2, thr = peer_topk(q, keys_bf16)
    return peer_dense(x, s1, s2, thr, u_bf16, vt_bf16, g, b)


def _pad_cols(w, mult=LANES):
    n = w.shape[-1]
    return jnp.pad(w, ((0, 0),) * (w.ndim - 1) + ((0, _round_up(n, mult) - n),))


def _rope_table(pos):
    half = C_ROPE // 2
    inv = ROPE_THETA ** (-jnp.arange(half, dtype=F32) / half)
    ang = pos.astype(F32)[:, None] * inv[None, :]
    cos = jnp.tile(jnp.cos(ang), (1, 2 * C_HEADS))
    sin = jnp.tile(jnp.sin(ang), (1, 2 * C_HEADS))
    return jnp.concatenate([cos, sin], axis=1)


def _mixer_a(x, bsz, t, w_in, w_out, g, b, bufs):
    m, d = x.shape
    proj = matmul(x, w_in.astype(BF16))
    p3 = proj.reshape(bsz, t, A_GROUPS * A_GW)
    rows = []
    if bufs is None:
        os_, ls_ = [], []
        for gi in range(A_GROUPS):
            o, l = attn_prompt(p3, gi)
            os_.append(o)
            ls_.append(l)
            keep = min(A_WINDOWS[gi], t)
            rows.append(p3[:, t - keep:, gi * A_GW + A_HW:(gi + 1) * A_GW]
                        .reshape(bsz, keep, 2, A_HEADS, A_HEAD_DIM))
        x1 = a_out(os_, ls_, w_out.astype(BF16), x, g, b)
    else:
        o, l = attn_step(jnp.pad(p3, ((0, 0), (0, 8 - t), (0, 0))), bufs, t)
        for gi in range(A_GROUPS):
            rows.append(p3[:, :, gi * A_GW + A_HW:(gi + 1) * A_GW].reshape(bsz, t, 2, A_HEADS, A_HEAD_DIM))
        x1 = a_out(o.reshape(m, A_GROUPS * A_HW), l.reshape(m, A_GROUPS * A_HW), w_out.astype(BF16), x, g, b)
    return x1, rows


def _mixer_b(x, bsz, t, w_in, conv_w, a_log, dt_bias, norm_w, w_out, g, b, conv_buf, s0):
    proj = matmul(x, _pad_cols(w_in).astype(BF16))
    p3 = proj.reshape(bsz, t, proj.shape[1])
    if conv_buf is None:
        conv_buf = jnp.zeros((bsz, CONV_W - 1, 3 * B_WIDTH), F32)
    if s0 is None:
        s0 = jnp.zeros((bsz, B_HEADS, B_DK, B_DV), F32)
    new_buf = jnp.concatenate([conv_buf, p3[:, :, :3 * B_WIDTH]], axis=1)[:, t:]
    c = min(B_CHUNK, _round_up(t, 8))
    tp = _round_up(t, c)
    if tp != t:
        p3 = jnp.pad(p3, ((0, 0), (0, tp - t), (0, 0)))
    o, s_fin = gdn(p3, conv_buf, s0, conv_w, a_log, dt_bias, norm_w, t)
    o = o[:, :t].reshape(bsz * t, B_WIDTH)
    x1 = matmul_res_ln(o, w_out.astype(BF16), x, g, b)
    return x1, new_buf, s_fin


def _mixer_c(x, bsz, t, pos, w_in, q_norm, kv_norm, w_uq, w_uk, w_uv, w_out, g, b, pool, page_table):
    m = x.shape[0]
    proj = matmul(x, _pad_cols(w_in).astype(BF16))
    wq3 = w_uq.reshape(C_Q_LORA, C_HEADS, C_NOPE + C_ROPE)
    w_uq_perm = jnp.concatenate([wq3[:, :, :C_NOPE].reshape(C_Q_LORA, -1),
                                 wq3[:, :, C_NOPE:].reshape(C_Q_LORA, -1)], axis=1).astype(BF16)
    w_uk_t = w_uk.transpose(1, 2, 0).astype(BF16)
    tm = min(_row_tile(m), 256)
    table = _rope_table(pos)
    reps = max(tm // t, 1)
    cs = jnp.tile(table, (reps, 1))
    lat_new, qcat = mla_prep(proj, cs, q_norm, kv_norm, w_uq_perm, w_uk_t)
    if pool is None:
        w_uv_t = w_uv.transpose(1, 0, 2).astype(BF16)
        o = mla_flash(qcat, lat_new.reshape(bsz, t, C_LAT), w_uv_t, bsz, t)
    else:
        qrows = qcat.reshape(C_HEADS, bsz, t, C_LAT).transpose(1, 2, 0, 3).reshape(bsz, t * C_HEADS, C_LAT)
        w_uv_flat = w_uv.reshape(C_KV_LORA, C_HEADS * C_V).astype(BF16)
        o = mla_step(qrows, lat_new.reshape(bsz, t, C_LAT), pool, page_table, w_uv_flat)
        o = o.reshape(m, C_HEADS * C_V)
    x1 = matmul_res_ln(o, w_out.astype(BF16), x, g, b)
    return x1, lat_new.reshape(bsz, t, C_LAT)


def _run_trunk(x3, pos, prm, peer, cache):
    bsz, t, d = x3.shape
    x = x3.reshape(bsz * t, d)
    new_a = [[] for _ in range(A_GROUPS)]
    new_conv, new_rec, new_lat = [], [], []
    for i in range(DEPTH):
        kind, li = i % N_MIXERS, i // N_MIXERS
        g1, b1 = prm['ln1_g'][i], prm['ln1_b'][i]
        if kind == 0:
            bufs = None if cache is None else [w[li] for w in cache['a_win']]
            x, rows = _mixer_a(x, bsz, t, prm['w_a_in'][li], prm['w_a_out'][li], g1, b1, bufs)
            for gi in range(A_GROUPS):
                new_a[gi].append(rows[gi])
        elif kind == 1:
            conv_buf = None if cache is None else cache['b_conv'][li]
            s0 = None if cache is None else cache['b_rec'][li]
            x, cb, st = _mixer_b(x, bsz, t, prm['w_b_in'][li], prm['b_conv_w'][li], prm['b_a_log'][li],
                                 prm['b_dt_bias'][li], prm['b_norm_w'][li], prm['w_b_out'][li], g1, b1,
                                 conv_buf, s0)
            new_conv.append(cb)
            new_rec.append(st)
        else:
            pool = None if cache is None else cache['c_pool'][li]
            pt = None if cache is None else cache['page_table']
            x, lat = _mixer_c(x, bsz, t, pos, prm['w_c_in'][li], prm['c_q_norm'][li], prm['c_kv_norm'][li],
                              prm['w_c_uq'][li], prm['w_c_uk'][li], prm['w_c_uv'][li], prm['w_c_out'][li],
                              g1, b1, pool, pt)
            new_lat.append(lat)
        wq, keys, u, vt = peer[i]
        x = peer_layer(x, wq, keys, u, vt, prm['ln2_g'][i], prm['ln2_b'][i])
    return (x.reshape(bsz, t, d), [jnp.stack(r) for r in new_a], jnp.stack(new_conv), jnp.stack(new_rec),
            jnp.stack(new_lat))


def kernel(x_prompt, x_sample, cache_a_win0, cache_a_win1, cache_a_win2, state_b_conv, state_b_rec,
           cache_c_latent, page_table, w_a_in, w_a_out, w_b_in, b_conv_w, b_a_log, b_dt_bias, b_norm_w,
           w_b_out, w_c_in, c_q_norm, c_kv_norm, w_c_uq, w_c_uk, w_c_uv, w_c_out, ln1_g, ln1_b, ln2_g,
           ln2_b, peer_wq, peer_keys, peer_u, peer_v):
    prm = dict(w_a_in=w_a_in, w_a_out=w_a_out, w_b_in=w_b_in, b_conv_w=b_conv_w, b_a_log=b_a_log,
               b_dt_bias=b_dt_bias, b_norm_w=b_norm_w, w_b_out=w_b_out, w_c_in=w_c_in, c_q_norm=c_q_norm,
               c_kv_norm=c_kv_norm, w_c_uq=w_c_uq, w_c_uk=w_c_uk, w_c_uv=w_c_uv, w_c_out=w_c_out,
               ln1_g=ln1_g, ln1_b=ln1_b, ln2_g=ln2_g, ln2_b=ln2_b)
    peer = [(peer_wq[i].astype(BF16), peer_keys[i].astype(BF16), peer_u[i].astype(BF16),
             peer_v[i].astype(BF16).T) for i in range(DEPTH)]
    y_p, a_p, b_conv_p, b_rec_p, c_lat_p = _run_trunk(x_prompt, jnp.arange(x_prompt.shape[1]), prm, peer, None)
    past_len = page_table.shape[1] * cache_c_latent.shape[2]
    cache = dict(a_win=(cache_a_win0, cache_a_win1, cache_a_win2), b_conv=state_b_conv, b_rec=state_b_rec,
                 c_pool=cache_c_latent, page_table=page_table)
    y_s, a_s, b_conv_s, b_rec_s, c_lat_s = _run_trunk(
        x_sample, past_len + jnp.arange(x_sample.shape[1]), prm, peer, cache)
    return (y_p, y_s, a_p[0], a_p[1], a_p[2], a_s[0], a_s[1], a_s[2],
            b_conv_p, b_rec_p, b_conv_s, b_rec_s, c_lat_p, c_lat_s)
```

```python
import functools
import math

import numpy as np
import jax
import jax.numpy as jnp
from jax import lax
from jax.experimental import pallas as pl
from jax.experimental.pallas import tpu as pltpu

F32 = jnp.float32
BF16 = jnp.bfloat16
HI = lax.Precision.HIGHEST

DEPTH = 4
N_MIXERS = 3
DN_ALPHA = (2 * DEPTH) ** 0.25
LN_EPS = 1e-5
RMS_EPS = 1e-6

A_WINDOWS = (128, 512, 2048)
A_DILATIONS = (1, 4, 16)
A_GROUPS = 3
A_HEADS = 8
A_HEAD_DIM = 64
A_SPAN = 128
A_GW = 3 * A_HEADS * A_HEAD_DIM
A_HW = A_HEADS * A_HEAD_DIM
A_UNROLL = 4

B_HEADS = 8
B_DK = 128
B_DV = 128
B_WIDTH = B_HEADS * B_DK
CONV_W = 4
B_CHUNK = 64

C_HEADS = 8
C_NOPE = 128
C_ROPE = 64
C_V = 128
C_KV_LORA = 256
C_Q_LORA = 384
C_LAT = C_KV_LORA + C_ROPE
C_SCALE = (C_NOPE + C_ROPE) ** -0.5
ROPE_THETA = 10000.0

P_HEADS = 8
N_KEYS = 128
P_QDIM = 256
P_TOPK = 16

LANES = 128
VMEM_LIMIT = 56 * 1024 * 1024
NEG_BIG = -1e30
LOG2E = 1.4426950408889634


def _cparams(*sem):
    return pltpu.CompilerParams(dimension_semantics=sem, vmem_limit_bytes=VMEM_LIMIT)


def _round_up(n, m):
    return -(-n // m) * m


def _row_tile(m):
    for t in (512, 256, 128, 64, 32, 16, 8):
        if m % t == 0:
            return t
    raise ValueError(m)


def _layer_norm(v, g, b):
    mu = jnp.mean(v, -1, keepdims=True)
    d = v - mu
    var = jnp.mean(d * d, -1, keepdims=True)
    return d * lax.rsqrt(var + LN_EPS) * g + b


def _bdot(a, b):
    return jnp.dot(a.astype(BF16), b.astype(BF16), preferred_element_type=F32)


def _bdot_nt(a, b):
    return lax.dot_general(a.astype(BF16), b.astype(BF16), (((1,), (1,)), ((), ())),
                           preferred_element_type=F32)


def _fdot(a, b):
    return jnp.dot(a, b, precision=HI, preferred_element_type=F32)


def _softplus(v):
    return jnp.maximum(v, 0.0) + jnp.log1p(jnp.exp(-jnp.abs(v)))


def _mm_kernel(x_ref, w_ref, o_ref):
    o_ref[...] = _bdot(x_ref[...], w_ref[...])


def matmul(x, w):
    m, k = x.shape
    n = w.shape[1]
    tm = min(_row_tile(m), 256)
    return pl.pallas_call(
        _mm_kernel,
        grid=(m // tm,),
        in_specs=[pl.BlockSpec((tm, k), lambda i: (i, 0)),
                  pl.BlockSpec((k, n), lambda i: (0, 0))],
        out_specs=pl.BlockSpec((tm, n), lambda i: (i, 0)),
        out_shape=jax.ShapeDtypeStruct((m, n), F32),
        compiler_params=_cparams("parallel"),
        name="matmul",
    )(x, w)


def _mm_res_ln_kernel(h_ref, w_ref, x_ref, g_ref, b_ref, o_ref):
    y = _bdot(h_ref[...], w_ref[...])
    o_ref[...] = _layer_norm(DN_ALPHA * x_ref[...] + y, g_ref[...], b_ref[...])


def matmul_res_ln(h, w, x, g, b):
    m, k = h.shape
    n = w.shape[1]
    tm = min(_row_tile(m), 256)
    return pl.pallas_call(
        _mm_res_ln_kernel,
        grid=(m // tm,),
        in_specs=[pl.BlockSpec((tm, k), lambda i: (i, 0)),
                  pl.BlockSpec((k, n), lambda i: (0, 0)),
                  pl.BlockSpec((tm, n), lambda i: (i, 0)),
                  pl.BlockSpec((1, n), lambda i: (0, 0)),
                  pl.BlockSpec((1, n), lambda i: (0, 0))],
        out_specs=pl.BlockSpec((tm, n), lambda i: (i, 0)),
        out_shape=jax.ShapeDtypeStruct((m, n), F32),
        compiler_params=_cparams("parallel"),
        name="matmul_res_ln",
    )(h, w, x, g.reshape(1, n), b.reshape(1, n))


def _pair_attention(items):
    lane = lax.broadcasted_iota(jnp.int32, (A_SPAN, LANES), 1)
    low = lane < A_HEAD_DIM
    sels = (low, jnp.logical_not(low))
    scores = [[jnp.where(valid, _bdot_nt(jnp.where(sel, q, 0.0), k) * (A_HEAD_DIM ** -0.5), -jnp.inf)
               for sel in sels] for q, k, _, valid in items]
    probs, lses = [], []
    for pair in scores:
        ps, ls = [], []
        for s in pair:
            mx = jnp.max(s, -1, keepdims=True)
            e = jnp.exp(s - mx)
            den = jnp.sum(e, -1, keepdims=True)
            ps.append(e * (1.0 / den))
            ls.append(jnp.broadcast_to(mx + jnp.log(den), (A_SPAN, LANES)))
        probs.append(ps)
        lses.append(ls)
    outs = [[_bdot(p, item[2]) for p in ps] for ps, item in zip(probs, items)]
    return [(jnp.where(low, o[0], o[1]), jnp.where(low, l[0], l[1])) for o, l in zip(outs, lses)]


def _attn_prompt_kernel(*refs, t):
    qkv = refs[0:3 * A_GROUPS]
    out_ref, o_scr, l_scr = refs[3 * A_GROUPS:]
    qi = lax.broadcasted_iota(jnp.int32, (A_SPAN, 2 * A_SPAN), 0) + A_SPAN
    ki = lax.broadcasted_iota(jnp.int32, (A_SPAN, 2 * A_SPAN), 1)
    band = (qi - ki >= 0) & (qi - ki <= A_SPAN)
    causal = band[:, A_SPAN:]
    for g in range(A_GROUPS):
        q_ref, k_ref, v_ref = qkv[3 * g:3 * g + 3]
        dil = A_DILATIONS[g]
        nb = t // dil // A_SPAN
        if dil == 1:
            def body(n4, carry):
                items, dst = [], []
                for u in range(A_UNROLL):
                    n = n4 * A_UNROLL + u
                    row0 = pl.multiple_of(n * A_SPAN, A_SPAN)
                    prev0 = pl.multiple_of(jnp.maximum(n - 1, 0) * A_SPAN, A_SPAN)
                    kk = jnp.concatenate([k_ref[pl.ds(prev0, A_SPAN), :], k_ref[pl.ds(row0, A_SPAN), :]], axis=0)
                    vv = jnp.concatenate([v_ref[pl.ds(prev0, A_SPAN), :], v_ref[pl.ds(row0, A_SPAN), :]], axis=0)
                    valid = band & ((ki >= A_SPAN) | (n > 0))
                    items.append((q_ref[pl.ds(row0, A_SPAN), :], kk.astype(BF16), vv.astype(BF16), valid))
                    dst.append(pl.ds(row0, A_SPAN))
                for rows, (o, l) in zip(dst, _pair_attention(items)):
                    o_scr[g, rows, :] = o
                    l_scr[g, rows, :] = l
                return carry
            lax.fori_loop(0, nb // A_UNROLL, body, 0)
            continue
        units = [(r, n) for r in range(dil) for n in range(nb)]
        for u0 in range(0, len(units), A_UNROLL):
            items, dst = [], []
            for r, n in units[u0:u0 + A_UNROLL]:
                cur = pl.ds(n * A_SPAN * dil + r, A_SPAN, stride=dil)
                kk, vv, valid = k_ref[cur, :], v_ref[cur, :], causal
                if n > 0:
                    prev = pl.ds((n - 1) * A_SPAN * dil + r, A_SPAN, stride=dil)
                    kk = jnp.concatenate([k_ref[prev, :], kk], axis=0)
                    vv = jnp.concatenate([v_ref[prev, :], vv], axis=0)
                    valid = band
                items.append((q_ref[cur, :], kk.astype(BF16), vv.astype(BF16), valid))
                dst.append(cur)
            for rows, (o, l) in zip(dst, _pair_attention(items)):
                o_scr[g, rows, :] = o
                l_scr[g, rows, :] = l
    chunk = 2 * A_SPAN
    for c0 in range(0, t, chunk):
        rows = slice(c0, c0 + chunk)
        ls = [l_scr[g, rows, :] for g in range(A_GROUPS)]
        mx = jnp.maximum(jnp.maximum(ls[0], ls[1]), ls[2])
        es = [jnp.exp(l - mx) for l in ls]
        inv = 1.0 / (es[0] + es[1] + es[2])
        out_ref[rows, :] = ((es[0] * inv) * o_scr[0, rows, :] + (es[1] * inv) * o_scr[1, rows, :]
                            + (es[2] * inv) * o_scr[2, rows, :])


def attn_prompt(proj):
    b, t, w = proj.shape
    assert t % (A_SPAN * max(A_DILATIONS)) == 0
    npair = A_HW // LANES
    specs = []
    for g in range(A_GROUPS):
        for c in range(3):
            col = (g * A_GW + c * A_HW) // LANES
            specs.append(pl.BlockSpec((None, t, LANES), functools.partial(
                lambda bi, p, col: (bi, 0, col + p), col=col)))
    kern = functools.partial(_attn_prompt_kernel, t=t)
    o = pl.pallas_call(
        kern,
        grid=(b, npair),
        in_specs=specs,
        out_specs=pl.BlockSpec((None, t, LANES), lambda bi, p: (bi, 0, p)),
        out_shape=jax.ShapeDtypeStruct((b, t, A_HW), F32),
        scratch_shapes=[pltpu.VMEM((A_GROUPS, t, LANES), F32), pltpu.VMEM((A_GROUPS, t, LANES), F32)],
        compiler_params=_cparams("parallel", "parallel"),
        name="attn_prompt",
    )(*([proj] * (3 * A_GROUPS)))
    return o.reshape(b * t, A_HW)


def _attn_step_kernel(qkv_ref, c0_ref, c1_ref, c2_ref, o_ref, l_ref, *, ts):
    caches = (c0_ref, c1_ref, c2_ref)
    jrow = lax.broadcasted_iota(jnp.int32, (A_SPAN, A_HEADS, 1), 0)
    scale = A_HEAD_DIM ** -0.5
    for g in range(A_GROUPS):
        dil = A_DILATIONS[g]
        for t in range(ts):
            q = qkv_ref[t, 3 * g]
            res = 0 if dil == 1 else t
            kbuf = caches[g][:, res, 0]
            vbuf = caches[g][:, res, 1]
            s = jnp.sum(kbuf * q[None], -1, keepdims=True) * scale
            if dil == 1:
                s = jnp.where(jrow >= t, s, -jnp.inf)
                new = list(range(t + 1))
            else:
                new = [t]
            s_new = [jnp.sum(qkv_ref[tn, 3 * g + 1] * q, -1, keepdims=True) * scale for tn in new]
            mx = jnp.max(s, 0)
            for sn in s_new:
                mx = jnp.maximum(mx, sn)
            e = jnp.exp(s - mx[None])
            den = jnp.sum(e, 0)
            acc = jnp.sum(e * vbuf, 0)
            for tn, sn in zip(new, s_new):
                en = jnp.exp(sn - mx)
                den = den + en
                acc = acc + en * qkv_ref[tn, 3 * g + 2]
            o_ref[t, g] = acc * (1.0 / den)
            l_ref[t, g] = jnp.broadcast_to(mx + jnp.log(den), (A_HEADS, A_HEAD_DIM))


def attn_step(qkv, caches, li):
    b, ts = qkv.shape[0], qkv.shape[1]
    views, specs = [], []
    for g in range(A_GROUPS):
        dil = A_DILATIONS[g]
        n_a, _, lb = caches[g].shape[:3]
        assert lb == A_SPAN * dil and (dil == 1 or ts <= dil)
        views.append(caches[g].reshape(n_a, b, A_SPAN, dil, 2, A_HEADS, A_HEAD_DIM))
        nres = 1 if dil == 1 else ts
        specs.append(pl.BlockSpec((None, None, A_SPAN, nres, 2, A_HEADS, A_HEAD_DIM),
                                  lambda bi: (li, bi, 0, 0, 0, 0, 0)))
    kern = functools.partial(_attn_step_kernel, ts=ts)
    oshape = (b, ts, A_GROUPS, A_HEADS, A_HEAD_DIM)
    ospec = pl.BlockSpec((None, ts, A_GROUPS, A_HEADS, A_HEAD_DIM), lambda bi: (bi, 0, 0, 0, 0))
    return pl.pallas_call(
        kern,
        grid=(b,),
        in_specs=[pl.BlockSpec((None, ts, 3 * A_GROUPS, A_HEADS, A_HEAD_DIM), lambda bi: (bi, 0, 0, 0, 0))] + specs,
        out_specs=[ospec, ospec],
        out_shape=[jax.ShapeDtypeStruct(oshape, F32)] * 2,
        compiler_params=_cparams("parallel"),
        name="attn_step",
    )(qkv, *views)


def _a_out_kernel(o0, o1, o2, l0, l1, l2, w_ref, x_ref, g_ref, b_ref, out_ref):
    ls = (l0[...], l1[...], l2[...])
    mx = jnp.maximum(jnp.maximum(ls[0], ls[1]), ls[2])
    es = [jnp.exp(l - mx) for l in ls]
    inv = 1.0 / (es[0] + es[1] + es[2])
    o = (es[0] * inv) * o0[...] + (es[1] * inv) * o1[...] + (es[2] * inv) * o2[...]
    y = _bdot(o, w_ref[...])
    out_ref[...] = _layer_norm(DN_ALPHA * x_ref[...] + y, g_ref[...], b_ref[...])


def a_out(os_, ls_, w, x, g, b):
    m, d = x.shape
    tm = min(_row_tile(m), 256)
    ol = [os_] * 3 + [ls_] * 3
    ospecs = [pl.BlockSpec((tm, A_HW), functools.partial(lambda i, gg: (i, gg), gg=gg))
              for gg in range(3)] * 2
    return pl.pallas_call(
        _a_out_kernel,
        grid=(m // tm,),
        in_specs=ospecs + [pl.BlockSpec((A_HW, d), lambda i: (0, 0)),
                           pl.BlockSpec((tm, d), lambda i: (i, 0)),
                           pl.BlockSpec((1, d), lambda i: (0, 0)),
                           pl.BlockSpec((1, d), lambda i: (0, 0))],
        out_specs=pl.BlockSpec((tm, d), lambda i: (i, 0)),
        out_shape=jax.ShapeDtypeStruct((m, d), F32),
        compiler_params=_cparams("parallel"),
        name="a_out",
    )(*ol, w, x, g.reshape(1, d), b.reshape(1, d))


def _split_bf16(a):
    hi = a.astype(BF16)
    lo = (a - hi.astype(F32)).astype(BF16)
    return hi, lo


def _dot3(a, b):
    ah, al = _split_bf16(a)
    bh, bl = _split_bf16(b)
    d = functools.partial(jnp.dot, preferred_element_type=F32)
    return d(ah, bh) + (d(ah, bl) + d(al, bh))


def _gdn_kernel(proj_ref, abt_ref, cbuf_ref, s0_ref, convw_ref, alog_row_ref, dt_row_ref, alog_col_ref,
                dt_col_ref, normw_ref, o_ref, s_ref, xp_ref, *, c, t_valid):
    ci = pl.program_id(1)
    qkv_w = 3 * B_WIDTH

    @pl.when(ci == 0)
    def _():
        s_ref[...] = s0_ref[...]
        xp_ref[8 - (CONV_W - 1):8, :] = cbuf_ref[0]

    xp_ref[8:8 + c, :] = proj_ref[0, :, 0:qkv_w]
    y = None
    for j in range(CONV_W):
        lo = 8 - (CONV_W - 1) + j
        term = xp_ref[lo:lo + c, :] * convw_ref[j:j + 1, :]
        y = term if y is None else y + term
    xp_ref[8 - (CONV_W - 1):8, :] = xp_ref[8 + c - (CONV_W - 1):8 + c, :]
    qkv = y * jax.nn.sigmoid(y)

    ab = proj_ref[0, :, 4 * B_WIDTH:4 * B_WIDTH + LANES]
    trow = lax.broadcasted_iota(jnp.int32, (c, LANES), 0) + ci * c
    live = trow < t_valid
    beta = jnp.where(live, jax.nn.sigmoid(ab), 0.0)
    gate = jnp.where(live, -jnp.exp(alog_row_ref[...]) * _softplus(ab + dt_row_ref[...]), 0.0)
    abt = abt_ref[0, 0]
    tcol = lax.broadcasted_iota(jnp.int32, (B_HEADS, c), 1) + ci * c
    gate_t = jnp.where(tcol < t_valid,
                       -jnp.exp(alog_col_ref[...]) * _softplus(abt[B_HEADS:2 * B_HEADS] + dt_col_ref[...]), 0.0)
    ii = lax.broadcasted_iota(jnp.int32, (c, c), 0)
    jj = lax.broadcasted_iota(jnp.int32, (c, c), 1)
    tri = ii >= jj
    stri = ii > jj
    gcum = _fdot(tri.astype(F32), gate)
    gcum_t = _fdot(gate_t, (ii <= jj).astype(F32))
    eye = (ii == jj).astype(F32)

    heads = range(B_HEADS)
    q_, k_, kb_, vb_, gc_, glast_, a_, qk_ = [], [], [], [], [], [], [], []
    for h in heads:
        qh = qkv[:, h * B_DK:(h + 1) * B_DK]
        kh = qkv[:, B_WIDTH + h * B_DK:B_WIDTH + (h + 1) * B_DK]
        vh = qkv[:, 2 * B_WIDTH + h * B_DV:2 * B_WIDTH + (h + 1) * B_DV]
        qh = qh * lax.rsqrt(jnp.sum(qh * qh, -1, keepdims=True) + 1e-6) * (B_DK ** -0.5)
        kh = kh * lax.rsqrt(jnp.sum(kh * kh, -1, keepdims=True) + 1e-6)
        bh = beta[:, h:h + 1]
        q_.append(qh)
        k_.append(kh)
        kb_.append(kh * bh)
        vb_.append(vh * bh)
        gc_.append(gcum[:, B_HEADS + h:B_HEADS + h + 1])
        glast_.append(gcum[c - 1:c, B_HEADS + h:B_HEADS + h + 1])
    kq_ = [_bdot_nt(jnp.concatenate([kb_[h], q_[h]], axis=0), k_[h]) for h in heads]
    for h in heads:
        dmat = jnp.exp(jnp.where(tri, gc_[h] - gcum_t[h:h + 1, :], -jnp.inf))
        a_.append(jnp.where(stri, kq_[h][0:c] * dmat, 0.0))
        qk_.append(jnp.where(tri, kq_[h][c:2 * c] * dmat, 0.0))
    mpow_ = [-a_[h] for h in heads]
    tinv_ = [eye + mpow_[h] for h in heads]
    for _ in range(int(math.log2(c)) - 1):
        mpow_ = [_dot3(mpow_[h], mpow_[h]) for h in heads]
        tinv_ = [tinv_[h] + _dot3(tinv_[h], mpow_[h]) for h in heads]
    egc_ = [jnp.exp(gc_[h]) for h in heads]
    wu_ = [_bdot(tinv_[h], jnp.concatenate([kb_[h] * egc_[h], vb_[h]], axis=1)) for h in heads]
    s_ = [s_ref[0, h] for h in heads]
    ws_ = [_bdot(jnp.concatenate([wu_[h][:, 0:B_DK], q_[h] * egc_[h]], axis=0), s_[h]) for h in heads]
    vnew_ = [wu_[h][:, B_DK:B_DK + B_DV] - ws_[h][0:c] for h in heads]
    o_ = [ws_[h][c:2 * c] + _bdot(qk_[h], vnew_[h]) for h in heads]
    for h in heads:
        kdec = k_[h] * jnp.exp(glast_[h] - gc_[h])
        upd = lax.dot_general(kdec.astype(BF16), vnew_[h].astype(BF16), (((0,), (0,)), ((), ())),
                              preferred_element_type=F32)
        s_ref[0, h] = s_[h] * jnp.exp(glast_[h]) + upd
    for h in heads:
        zh = proj_ref[0, :, qkv_w + h * B_DV:qkv_w + (h + 1) * B_DV]
        on = o_[h] * lax.rsqrt(jnp.mean(o_[h] * o_[h], -1, keepdims=True) + RMS_EPS) * normw_ref[...]
        o_ref[0, :, h * B_DV:(h + 1) * B_DV] = on * (zh * jax.nn.sigmoid(zh))


def gdn(proj, conv_buf, s0, conv_w, a_log, dt_bias, norm_w, t_valid):
    b, t, w = proj.shape
    c = min(B_CHUNK, t)
    n = t // c
    ab = proj[:, :, 4 * B_WIDTH:4 * B_WIDTH + 2 * B_HEADS]
    abt = ab.reshape(b, n, c, 2 * B_HEADS).transpose(0, 1, 3, 2)
    lane_pad = LANES - 2 * B_HEADS
    alog_row = jnp.pad(a_log.reshape(1, B_HEADS), ((0, 0), (B_HEADS, lane_pad)))
    dt_row = jnp.pad(dt_bias.reshape(1, B_HEADS), ((0, 0), (B_HEADS, lane_pad)))
    kern = functools.partial(_gdn_kernel, c=c, t_valid=t_valid)
    const2 = lambda bi, ci: (0, 0)
    return pl.pallas_call(
        kern,
        grid=(b, n),
        in_specs=[pl.BlockSpec((1, c, w), lambda bi, ci: (bi, ci, 0)),
                  pl.BlockSpec((1, 1, 2 * B_HEADS, c), lambda bi, ci: (bi, ci, 0, 0)),
                  pl.BlockSpec((1, CONV_W - 1, 3 * B_WIDTH), lambda bi, ci: (bi, 0, 0)),
                  pl.BlockSpec((1, B_HEADS, B_DK, B_DV), lambda bi, ci: (bi, 0, 0, 0)),
                  pl.BlockSpec((CONV_W, 3 * B_WIDTH), const2),
                  pl.BlockSpec((1, LANES), const2),
                  pl.BlockSpec((1, LANES), const2),
                  pl.BlockSpec((B_HEADS, 1), const2),
                  pl.BlockSpec((B_HEADS, 1), const2),
                  pl.BlockSpec((1, B_DV), const2)],
        out_specs=[pl.BlockSpec((1, c, B_WIDTH), lambda bi, ci: (bi, ci, 0)),
                   pl.BlockSpec((1, B_HEADS, B_DK, B_DV), lambda bi, ci: (bi, 0, 0, 0))],
        out_shape=[jax.ShapeDtypeStruct((b, t, B_WIDTH), F32),
                   jax.ShapeDtypeStruct((b, B_HEADS, B_DK, B_DV), F32)],
        scratch_shapes=[pltpu.VMEM((8 + c, 3 * B_WIDTH), F32)],
        compiler_params=_cparams("parallel", "arbitrary"),
        name="gdn",
    )(proj, abt, conv_buf, s0, conv_w, alog_row, dt_row, a_log.reshape(B_HEADS, 1),
      dt_bias.reshape(B_HEADS, 1), norm_w.reshape(1, B_DV))


def _rot_half(x, width):
    lane = lax.broadcasted_iota(jnp.int32, x.shape, 1)
    first = (lane % C_ROPE) < C_ROPE // 2
    return jnp.where(first, -pltpu.roll(x, width - C_ROPE // 2, 1), pltpu.roll(x, C_ROPE // 2, 1))


def _mla_prep_kernel(proj_ref, cs_ref, qn_ref, kvn_ref, wuq_ref, wuk_ref, lat_ref, qcat_ref):
    rw = C_HEADS * C_ROPE
    cos_all = cs_ref[:, 0:rw]
    sin_all = cs_ref[:, rw:2 * rw]
    pc = proj_ref[...]
    cq = pc[:, 0:C_Q_LORA]
    cq = cq * lax.rsqrt(jnp.mean(cq * cq, -1, keepdims=True) + RMS_EPS) * qn_ref[...]
    ckv = pc[:, C_Q_LORA:C_Q_LORA + C_KV_LORA]
    ckv = ckv * lax.rsqrt(jnp.mean(ckv * ckv, -1, keepdims=True) + RMS_EPS) * kvn_ref[...]
    lat_ref[:, 0:C_KV_LORA] = ckv
    kr = pc[:, C_Q_LORA + C_KV_LORA:C_Q_LORA + C_KV_LORA + LANES]
    kr = kr * cos_all[:, 0:LANES] + _rot_half(kr, LANES) * sin_all[:, 0:LANES]
    lat_ref[:, C_KV_LORA:C_LAT] = kr[:, 0:C_ROPE]
    q = _bdot(cq, wuq_ref[...])
    nw = C_HEADS * C_NOPE
    qr = q[:, nw:nw + rw]
    qr = qr * cos_all + _rot_half(qr, rw) * sin_all
    for h in range(C_HEADS):
        qcat_ref[h, :, 0:C_KV_LORA] = _bdot(q[:, h * C_NOPE:(h + 1) * C_NOPE], wuk_ref[h])
        qcat_ref[h, :, C_KV_LORA:C_LAT] = qr[:, h * C_ROPE:(h + 1) * C_ROPE]


def mla_prep(proj, cs, q_norm, kv_norm, w_uq_perm, w_uk_t):
    m, w = proj.shape
    tm = min(_row_tile(m), 256)
    nrep = cs.shape[0] // tm
    const2 = lambda i: (0, 0)
    return pl.pallas_call(
        _mla_prep_kernel,
        grid=(m // tm,),
        in_specs=[pl.BlockSpec((tm, w), lambda i: (i, 0)),
                  pl.BlockSpec((tm, cs.shape[1]), lambda i: (i % nrep, 0)),
                  pl.BlockSpec((1, C_Q_LORA), const2),
                  pl.BlockSpec((1, C_KV_LORA), const2),
                  pl.BlockSpec(w_uq_perm.shape, const2),
                  pl.BlockSpec(w_uk_t.shape, lambda i: (0, 0, 0))],
        out_specs=[pl.BlockSpec((tm, C_LAT), lambda i: (i, 0)),
                   pl.BlockSpec((C_HEADS, tm, C_LAT), lambda i: (0, i, 0))],
        out_shape=[jax.ShapeDtypeStruct((m, C_LAT), F32),
                   jax.ShapeDtypeStruct((C_HEADS, m, C_LAT), F32)],
        compiler_params=_cparams("parallel"),
        name="mla_prep",
    )(proj, cs, q_norm.reshape(1, -1), kv_norm.reshape(1, -1), w_uq_perm, w_uk_t)


def _mla_flash_kernel(q_ref, lat_ref, wuv_ref, o_ref, m_ref, l_ref, acc_ref, *, tq, tk):
    qi = pl.program_id(1)
    ki = pl.program_id(2)
    last = (qi * tq + tq - 1) // tk
    rows = C_HEADS * tq

    @pl.when(ki == 0)
    def _():
        m_ref[...] = jnp.full_like(m_ref, -jnp.inf)
        l_ref[...] = jnp.zeros_like(l_ref)
        acc_ref[...] = jnp.zeros_like(acc_ref)

    @pl.when(ki <= last)
    def _():
        lat = lat_ref[0].astype(BF16)
        qpos = qi * tq + lax.broadcasted_iota(jnp.int32, (tq, tk), 0)
        kpos = ki * tk + lax.broadcasted_iota(jnp.int32, (tq, tk), 1)
        visible = (kpos <= qpos)[None]
        ngrp = 2
        hg = C_HEADS // ngrp
        gr = [slice(i * hg * tq, (i + 1) * hg * tq) for i in range(ngrp)]
        ss = [_bdot_nt(q_ref[i * hg:(i + 1) * hg].reshape(hg * tq, C_LAT), lat) * C_SCALE for i in range(ngrp)]
        ss = [jnp.where(visible, s.reshape(hg, tq, tk), -jnp.inf).reshape(hg * tq, tk) for s in ss]
        ps, alphas = [], []
        for i, s in enumerate(ss):
            m_old = m_ref[gr[i], :]
            m_new = jnp.maximum(m_old, jnp.max(s, -1, keepdims=True))
            alpha = jnp.exp(m_old - m_new)
            p = jnp.exp(s - m_new)
            l_ref[gr[i], :] = alpha * l_ref[gr[i], :] + jnp.sum(p, -1, keepdims=True)
            m_ref[gr[i], :] = m_new
            ps.append(p)
            alphas.append(alpha)
        for i in range(ngrp):
            acc_ref[gr[i], :] = alphas[i] * acc_ref[gr[i], :] + _bdot(ps[i], lat[:, 0:C_KV_LORA])

    @pl.when(ki == last)
    def _():
        o = acc_ref[...] * (1.0 / l_ref[...])
        for h in range(C_HEADS):
            o_ref[:, h * C_V:(h + 1) * C_V] = _bdot(o[h * tq:(h + 1) * tq], wuv_ref[h])


def mla_flash(qcat, lat, w_uv_t, b, t):
    tq, tk = 128, 512
    tk = min(tk, t)
    tq = min(tq, t)
    nq, nk = t // tq, t // tk
    kern = functools.partial(_mla_flash_kernel, tq=tq, tk=tk)
    rows = C_HEADS * tq
    return pl.pallas_call(
        kern,
        grid=(b, nq, nk),
        in_specs=[pl.BlockSpec((C_HEADS, tq, C_LAT), lambda bi, qi, ki: (0, bi * nq + qi, 0)),
                  pl.BlockSpec((1, tk, C_LAT),
                               lambda bi, qi, ki: (bi, jnp.minimum(ki, (qi * tq + tq - 1) // tk), 0)),
                  pl.BlockSpec(w_uv_t.shape, lambda bi, qi, ki: (0, 0, 0))],
        out_specs=pl.BlockSpec((tq, C_HEADS * C_V), lambda bi, qi, ki: (bi * nq + qi, 0)),
        out_shape=jax.ShapeDtypeStruct((b * t, C_HEADS * C_V), F32),
        scratch_shapes=[pltpu.VMEM((rows, 1), F32), pltpu.VMEM((rows, 1), F32),
                        pltpu.VMEM((rows, C_KV_LORA), F32)],
        compiler_params=_cparams("parallel", "parallel", "arbitrary"),
        name="mla_flash",
    )(qcat, lat, w_uv_t)


def _mla_step_kernel(pt_ref, q_ref, new_ref, *rest, ts, npg):
    page_refs = rest[:npg]
    wuv_ref, o_ref, m_ref, l_ref, acc_ref = rest[npg:]
    si = pl.program_id(1)
    rows = ts * C_HEADS
    q = q_ref[0]

    @pl.when(si == 0)
    def _():
        new = new_ref[0]
        tq_ = lax.broadcasted_iota(jnp.int32, (rows, 1), 0) // C_HEADS
        s_new = []
        for tn in range(ts):
            sn = jnp.sum(q * new[tn:tn + 1, :], -1, keepdims=True) * C_SCALE
            s_new.append(jnp.where(tq_ >= tn, sn, -jnp.inf))
        mx = s_new[0]
        for sn in s_new[1:]:
            mx = jnp.maximum(mx, sn)
        den = jnp.zeros((rows, 1), F32)
        acc = jnp.zeros((rows, C_KV_LORA), F32)
        for tn in range(ts):
            e = jnp.exp(s_new[tn] - mx)
            den = den + e
            acc = acc + e * new[tn:tn + 1, 0:C_KV_LORA]
        m_ref[...] = mx
        l_ref[...] = den
        acc_ref[...] = acc

    s = jnp.concatenate([_bdot_nt(q, pr[0]) for pr in page_refs], axis=1) * C_SCALE
    m_old = m_ref[...]
    m_new = jnp.maximum(m_old, jnp.max(s, -1, keepdims=True))
    alpha = jnp.exp(m_old - m_new)
    p = jnp.exp(s - m_new)
    l_ref[...] = alpha * l_ref[...] + jnp.sum(p, -1, keepdims=True)
    acc = alpha * acc_ref[...]
    psz = page_refs[0].shape[1]
    for i, pr in enumerate(page_refs):
        acc = acc + _bdot(p[:, i * psz:(i + 1) * psz], pr[0, :, 0:C_KV_LORA])
    acc_ref[...] = acc
    m_ref[...] = m_new

    @pl.when(si == pl.num_programs(1) - 1)
    def _():
        o = acc_ref[...] * (1.0 / l_ref[...])
        full = _bdot(o, wuv_ref[...])
        hrow = lax.broadcasted_iota(jnp.int32, (C_HEADS, C_HEADS * C_V), 0)
        hcol = lax.broadcasted_iota(jnp.int32, (C_HEADS, C_HEADS * C_V), 1) // C_V
        diag = hrow == hcol
        for t in range(ts):
            blk = full[t * C_HEADS:(t + 1) * C_HEADS]
            o_ref[0, t:t + 1, :] = jnp.sum(jnp.where(diag, blk, 0.0), 0, keepdims=True)


def mla_step(qrows, lat_new, pool, page_table, w_uv_flat):
    b, rows, _ = qrows.shape
    ts = rows // C_HEADS
    n_pages = page_table.shape[1]
    psz = pool.shape[1]
    npg = 16
    while n_pages % npg:
        npg //= 2
    kern = functools.partial(_mla_step_kernel, ts=ts, npg=npg)

    def page_map(bi, si, pt, k):
        return (pt[bi, si * npg + k], 0, 0)

    grid_spec = pltpu.PrefetchScalarGridSpec(
        num_scalar_prefetch=1,
        grid=(b, n_pages // npg),
        in_specs=[pl.BlockSpec((1, rows, C_LAT), lambda bi, si, pt: (bi, 0, 0)),
                  pl.BlockSpec((1, ts, C_LAT), lambda bi, si, pt: (bi, 0, 0))]
                 + [pl.BlockSpec((1, psz, C_LAT), functools.partial(page_map, k=k)) for k in range(npg)]
                 + [pl.BlockSpec(w_uv_flat.shape, lambda bi, si, pt: (0, 0))],
        out_specs=pl.BlockSpec((1, ts, C_HEADS * C_V), lambda bi, si, pt: (bi, 0, 0)),
        scratch_shapes=[pltpu.VMEM((rows, 1), F32), pltpu.VMEM((rows, 1), F32),
                        pltpu.VMEM((rows, C_KV_LORA), F32)],
    )
    return pl.pallas_call(
        kern,
        grid_spec=grid_spec,
        out_shape=jax.ShapeDtypeStruct((b, ts, C_HEADS * C_V), F32),
        compiler_params=_cparams("parallel", "arbitrary"),
        name="mla_step",
    )(page_table, qrows, lat_new, *([pool] * npg), w_uv_flat)


P_RANKS = P_TOPK + 1


def _pruned_pairs():
    return [(i, j) for i in range(P_RANKS) for j in range(P_RANKS // (i + 1))]


def _extract_top(s, n):
    vals = []
    for _ in range(n):
        mx = jnp.max(s, 0, keepdims=True)
        vals.append(mx)
        s = jnp.where(s == mx, -jnp.inf, s)
    return vals


def _candidates(avals, bvals, tn):
    pairs = _pruned_pairs()
    sub = lax.broadcasted_iota(jnp.int32, (8, tn), 0)
    groups = []
    for g0 in range(0, len(pairs), 8):
        grp = pairs[g0:g0 + 8]
        av = jnp.broadcast_to(avals[grp[0][0]], (8, tn))
        bv = jnp.broadcast_to(bvals[grp[0][1]], (8, tn))
        for r in range(1, len(grp)):
            if grp[r][0] != grp[r - 1][0]:
                av = jnp.where(sub >= r, avals[grp[r][0]], av)
            if grp[r][1] != grp[r - 1][1]:
                bv = jnp.where(sub >= r, bvals[grp[r][1]], bv)
        cand = av + bv
        if len(grp) < 8:
            cand = jnp.where(sub < len(grp), cand, -jnp.inf)
        groups.append(cand)
    return jnp.concatenate(groups, axis=0)


def _peer_topk_kernel(q_ref, keys_ref, s1_ref, s2_ref, thr_ref):
    tn = q_ref.shape[0]
    half = P_QDIM // 2
    sub = lax.broadcasted_iota(jnp.int32, (P_HEADS, tn), 0)
    thr = jnp.zeros((P_HEADS, tn), F32)
    for h in range(P_HEADS):
        sc = []
        for c in range(2):
            qs = q_ref[:, h * P_QDIM + c * half:h * P_QDIM + (c + 1) * half]
            s = _bdot_nt(keys_ref[h, c], qs)
            sc.append(s - jnp.max(s, 0, keepdims=True))
        a = _extract_top(sc[0], P_RANKS)
        bvals = _extract_top(sc[1], P_RANKS)
        top = _extract_top(_candidates(a, bvals, tn), P_RANKS)
        z = jnp.exp(top[0])
        for v in top[1:P_TOPK]:
            z = z + jnp.exp(v)
        logz = jnp.log(z)
        s1_ref[h] = (sc[0] - logz) * LOG2E
        s2_ref[h] = sc[1] * LOG2E
        mid = 0.5 * (top[P_TOPK - 1] + top[P_TOPK])
        thr = jnp.where(sub == h, (mid - logz) * LOG2E, thr)
    thr_ref[...] = thr


def peer_topk(q, keys_bf16):
    m = q.shape[0]
    tn = LANES
    return pl.pallas_call(
        _peer_topk_kernel,
        grid=(m // tn,),
        in_specs=[pl.BlockSpec((tn, q.shape[1]), lambda i: (i, 0)),
                  pl.BlockSpec(keys_bf16.shape, lambda i: (0, 0, 0, 0))],
        out_specs=[pl.BlockSpec((P_HEADS, N_KEYS, tn), lambda i: (0, 0, i)),
                   pl.BlockSpec((P_HEADS, N_KEYS, tn), lambda i: (0, 0, i)),
                   pl.BlockSpec((P_HEADS, tn), lambda i: (0, i))],
        out_shape=[jax.ShapeDtypeStruct((P_HEADS, N_KEYS, m), F32),
                   jax.ShapeDtypeStruct((P_HEADS, N_KEYS, m), F32),
                   jax.ShapeDtypeStruct((P_HEADS, m), F32)],
        compiler_params=_cparams("parallel"),
        name="peer_topk",
    )(q, keys_bf16)


def _gelu(v):
    return 0.5 * v * (1.0 + lax.erf(v * (2.0 ** -0.5)))


PEER_I1 = 8
PEER_TE = PEER_I1 * N_KEYS
PEER_TN = 256


def _expert_hidden(u_ref, xb_ref, rows=slice(None)):
    return lax.dot_general(u_ref[rows, :], xb_ref[...], (((1,), (1,)), ((), ())), preferred_element_type=F32)


def _gate_stage(s1_ref, s2_ref, thr_ref, ht_ref, at_ref, i1_base, ci, between=()):
    sub = 8
    cols = pl.ds(pl.multiple_of(ci * LANES, LANES), LANES)
    s1_rows = [s1_ref[h, pl.ds(i1_base, PEER_I1), cols] for h in range(P_HEADS)]
    thr = thr_ref[:, cols]
    thr_b = [jnp.broadcast_to(thr[h:h + 1, :], (sub, LANES)) for h in range(P_HEADS)]
    between = list(between)
    every = PEER_I1 // max(len(between), 1)
    for j in range(PEER_I1):
        if between and j % every == 0:
            between.pop(0)()
        gsum = [jnp.zeros((sub, LANES), F32)] * (N_KEYS // sub)
        for h in range(P_HEADS):
            row = jnp.broadcast_to(s1_rows[h][j:j + 1, :], (sub, LANES))
            for v in range(N_KEYS // sub):
                ssum = row + s2_ref[h, v * sub:(v + 1) * sub, cols]
                gsum[v] = gsum[v] + jnp.exp2(jnp.where(ssum >= thr_b[h], ssum, NEG_BIG))
        rows = slice(j * N_KEYS, (j + 1) * N_KEYS)
        act = _gelu(ht_ref[rows, cols]) * jnp.concatenate(gsum, axis=0)
        at_ref[rows, cols] = act.astype(BF16)


def _peer_dense_kernel(x_ref, s1_ref, s2_ref, thr_ref, u0_ref, ua_ref, ub_ref, va_ref, vb_ref, vl_ref,
                       g_ref, b_ref, o_ref, xb_ref, ht0_ref, ht1_ref, at0_ref, at1_ref, acc_ref):
    si = pl.program_id(1)
    tm = x_ref.shape[0]
    nc = tm // LANES
    vdot = functools.partial(jnp.dot, preferred_element_type=F32)

    @pl.when(si == 0)
    def _():
        xb_ref[...] = x_ref[...].astype(BF16)
        acc_ref[...] = jnp.zeros_like(acc_ref)
        at1_ref[...] = jnp.zeros_like(at1_ref)
        ht0_ref[...] = _expert_hidden(u0_ref, xb_ref)

    te, d = ua_ref.shape

    def tile(u_next_ref, ht_next_ref, v_prev_ref, at_prev_ref, ht_ref, at_ref, i1_base):
        def body(ci, carry):
            er = pl.ds(pl.multiple_of(ci * (te // nc), te // nc), te // nc)
            dr = pl.ds(pl.multiple_of(ci * (d // nc), d // nc), d // nc)
            pieces = []
            tn = min(PEER_TN, tm)
            for n0 in range(0, tm, tn):
                tok = slice(n0, n0 + tn)

                def hidden(tok=tok):
                    ht_next_ref[er, tok] = lax.dot_general(
                        u_next_ref[er, :], xb_ref[tok, :], (((1,), (1,)), ((), ())), preferred_element_type=F32)

                def value(tok=tok):
                    acc_ref[dr, tok] += vdot(v_prev_ref[dr, :], at_prev_ref[:, tok])

                pieces += [hidden, value]
            _gate_stage(s1_ref, s2_ref, thr_ref, ht_ref, at_ref, i1_base, ci, pieces)
            return carry
        lax.fori_loop(0, nc, body, 0)

    i1_even = pl.multiple_of(si * (2 * PEER_I1), PEER_I1)

    tile(ua_ref, ht1_ref, va_ref, at1_ref, ht0_ref, at0_ref, i1_even)
    tile(ub_ref, ht0_ref, vb_ref, at0_ref, ht1_ref, at1_ref, i1_even + PEER_I1)

    @pl.when(si == pl.num_programs(1) - 1)
    def _():
        y = acc_ref[...] + vdot(vl_ref[...], at1_ref[...])
        o_ref[...] = _layer_norm(DN_ALPHA * x_ref[...] + y.T, g_ref[...], b_ref[...])


def peer_dense(x, s1, s2, thr, u_bf16, vt_bf16, g, b):
    m, d = x.shape
    nt = u_bf16.shape[0] // PEER_TE
    assert nt % 2 == 0
    tm = min(_row_tile(m), 512)
    last = nt - 1
    u_spec = lambda f: pl.BlockSpec((PEER_TE, d), lambda i, s: (f(s), 0))
    v_spec = lambda f: pl.BlockSpec((d, PEER_TE), lambda i, s: (0, f(s)))
    return pl.pallas_call(
        _peer_dense_kernel,
        grid=(m // tm, nt // 2),
        in_specs=[pl.BlockSpec((tm, d), lambda i, s: (i, 0)),
                  pl.BlockSpec((P_HEADS, N_KEYS, tm), lambda i, s: (0, 0, i)),
                  pl.BlockSpec((P_HEADS, N_KEYS, tm), lambda i, s: (0, 0, i)),
                  pl.BlockSpec((P_HEADS, tm), lambda i, s: (0, i)),
                  u_spec(lambda s: 0),
                  u_spec(lambda s: 2 * s + 1),
                  u_spec(lambda s: jnp.minimum(2 * s + 2, last)),
                  v_spec(lambda s: jnp.maximum(2 * s - 1, 0)),
                  v_spec(lambda s: 2 * s),
                  v_spec(lambda s: last),
                  pl.BlockSpec((1, d), lambda i, s: (0, 0)),
                  pl.BlockSpec((1, d), lambda i, s: (0, 0))],
        out_specs=pl.BlockSpec((tm, d), lambda i, s: (i, 0)),
        out_shape=jax.ShapeDtypeStruct((m, d), F32),
        scratch_shapes=[pltpu.VMEM((tm, d), BF16),
                        pltpu.VMEM((PEER_TE, tm), F32), pltpu.VMEM((PEER_TE, tm), F32),
                        pltpu.VMEM((PEER_TE, tm), BF16), pltpu.VMEM((PEER_TE, tm), BF16),
                        pltpu.VMEM((d, tm), F32)],
        compiler_params=_cparams("parallel", "arbitrary"),
        name="peer_dense",
    )(x, s1, s2, thr, u_bf16, u_bf16, u_bf16, vt_bf16, vt_bf16, vt_bf16, g.reshape(1, d), b.reshape(1, d))


def peer_layer(x, wq_bf16, keys_bf16, u_bf16, vt_bf16, g, b):
    q = matmul(x, wq_bf16)
    s1, s2, thr = peer_topk(q, keys_bf16)
    return peer_dense(x, s1, s2, thr, u_bf16, vt_bf16, g, b)


def _pad_cols(w, mult=LANES):
    n = w.shape[-1]
    return jnp.pad(w, ((0, 0),) * (w.ndim - 1) + ((0, _round_up(n, mult) - n),))


def _rope_table(pos):
    half = C_ROPE // 2
    inv = ROPE_THETA ** (-jnp.arange(half, dtype=F32) / half)
    ang = pos.astype(F32)[:, None] * inv[None, :]
    cos = jnp.tile(jnp.cos(ang), (1, 2 * C_HEADS))
    sin = jnp.tile(jnp.sin(ang), (1, 2 * C_HEADS))
    return jnp.concatenate([cos, sin], axis=1)


def _mixer_a(x, bsz, t, w_in, w_out, g, b, bufs, li):
    m, d = x.shape
    proj = matmul(x, w_in.astype(BF16))
    p3 = proj.reshape(bsz, t, A_GROUPS * A_GW)
    rows = []
    if bufs is None:
        for gi in range(A_GROUPS):
            keep = min(A_WINDOWS[gi], t)
            rows.append(p3[:, t - keep:, gi * A_GW + A_HW:(gi + 1) * A_GW]
                        .reshape(bsz, keep, 2, A_HEADS, A_HEAD_DIM))
        x1 = matmul_res_ln(attn_prompt(p3), w_out.astype(BF16), x, g, b)
    else:
        qkv = p3.reshape(bsz, t, 3 * A_GROUPS, A_HEADS, A_HEAD_DIM)
        o, l = attn_step(qkv, bufs, li)
        for gi in range(A_GROUPS):
            rows.append(qkv[:, :, 3 * gi + 1:3 * gi + 3])
        x1 = a_out(o.reshape(m, A_GROUPS * A_HW), l.reshape(m, A_GROUPS * A_HW), w_out.astype(BF16), x, g, b)
    return x1, rows


def _mixer_b(x, bsz, t, w_in, conv_w, a_log, dt_bias, norm_w, w_out, g, b, conv_buf, s0):
    proj = matmul(x, _pad_cols(w_in).astype(BF16))
    p3 = proj.reshape(bsz, t, proj.shape[1])
    if conv_buf is None:
        conv_buf = jnp.zeros((bsz, CONV_W - 1, 3 * B_WIDTH), F32)
    if s0 is None:
        s0 = jnp.zeros((bsz, B_HEADS, B_DK, B_DV), F32)
    new_buf = jnp.concatenate([conv_buf, p3[:, :, :3 * B_WIDTH]], axis=1)[:, t:]
    c = min(B_CHUNK, _round_up(t, 8))
    tp = _round_up(t, c)
    if tp != t:
        p3 = jnp.pad(p3, ((0, 0), (0, tp - t), (0, 0)))
    o, s_fin = gdn(p3, conv_buf, s0, conv_w, a_log, dt_bias, norm_w, t)
    o = o[:, :t].reshape(bsz * t, B_WIDTH)
    x1 = matmul_res_ln(o, w_out.astype(BF16), x, g, b)
    return x1, new_buf, s_fin


def _mixer_c(x, bsz, t, pos, w_in, q_norm, kv_norm, w_uq, w_uk, w_uv, w_out, g, b, pool, page_table):
    m = x.shape[0]
    proj = matmul(x, _pad_cols(w_in).astype(BF16))
    wq3 = w_uq.reshape(C_Q_LORA, C_HEADS, C_NOPE + C_ROPE)
    w_uq_perm = jnp.concatenate([wq3[:, :, :C_NOPE].reshape(C_Q_LORA, -1),
                                 wq3[:, :, C_NOPE:].reshape(C_Q_LORA, -1)], axis=1).astype(BF16)
    w_uk_t = w_uk.transpose(1, 2, 0).astype(BF16)
    tm = min(_row_tile(m), 256)
    table = _rope_table(pos)
    reps = max(tm // t, 1)
    cs = jnp.tile(table, (reps, 1))
    lat_new, qcat = mla_prep(proj, cs, q_norm, kv_norm, w_uq_perm, w_uk_t)
    if pool is None:
        w_uv_t = w_uv.transpose(1, 0, 2).astype(BF16)
        o = mla_flash(qcat, lat_new.reshape(bsz, t, C_LAT), w_uv_t, bsz, t)
    else:
        qrows = qcat.reshape(C_HEADS, bsz, t, C_LAT).transpose(1, 2, 0, 3).reshape(bsz, t * C_HEADS, C_LAT)
        w_uv_flat = w_uv.reshape(C_KV_LORA, C_HEADS * C_V).astype(BF16)
        o = mla_step(qrows, lat_new.reshape(bsz, t, C_LAT), pool, page_table, w_uv_flat)
        o = o.reshape(m, C_HEADS * C_V)
    x1 = matmul_res_ln(o, w_out.astype(BF16), x, g, b)
    return x1, lat_new.reshape(bsz, t, C_LAT)


def _run_trunk(x3, pos, prm, peer, cache):
    bsz, t, d = x3.shape
    x = x3.reshape(bsz * t, d)
    new_a = [[] for _ in range(A_GROUPS)]
    new_conv, new_rec, new_lat = [], [], []
    for i in range(DEPTH):
        kind, li = i % N_MIXERS, i // N_MIXERS
        g1, b1 = prm['ln1_g'][i], prm['ln1_b'][i]
        if kind == 0:
            bufs = None if cache is None else cache['a_win']
            x, rows = _mixer_a(x, bsz, t, prm['w_a_in'][li], prm['w_a_out'][li], g1, b1, bufs, li)
            for gi in range(A_GROUPS):
                new_a[gi].append(rows[gi])
        elif kind == 1:
            conv_buf = None if cache is None else cache['b_conv'][li]
            s0 = None if cache is None else cache['b_rec'][li]
            x, cb, st = _mixer_b(x, bsz, t, prm['w_b_in'][li], prm['b_conv_w'][li], prm['b_a_log'][li],
                                 prm['b_dt_bias'][li], prm['b_norm_w'][li], prm['w_b_out'][li], g1, b1,
                                 conv_buf, s0)
            new_conv.append(cb)
            new_rec.append(st)
        else:
            pool = None if cache is None else cache['c_pool'][li]
            pt = None if cache is None else cache['page_table']
            x, lat = _mixer_c(x, bsz, t, pos, prm['w_c_in'][li], prm['c_q_norm'][li], prm['c_kv_norm'][li],
                              prm['w_c_uq'][li], prm['w_c_uk'][li], prm['w_c_uv'][li], prm['w_c_out'][li],
                              g1, b1, pool, pt)
            new_lat.append(lat)
        wq, keys, u, vt = peer[i]
        x = peer_layer(x, wq, keys, u, vt, prm['ln2_g'][i], prm['ln2_b'][i])
    return (x.reshape(bsz, t, d), [jnp.stack(r) for r in new_a], jnp.stack(new_conv), jnp.stack(new_rec),
            jnp.stack(new_lat))


def kernel(x_prompt, x_sample, cache_a_win0, cache_a_win1, cache_a_win2, state_b_conv, state_b_rec,
           cache_c_latent, page_table, w_a_in, w_a_out, w_b_in, b_conv_w, b_a_log, b_dt_bias, b_norm_w,
           w_b_out, w_c_in, c_q_norm, c_kv_norm, w_c_uq, w_c_uk, w_c_uv, w_c_out, ln1_g, ln1_b, ln2_g,
           ln2_b, peer_wq, peer_keys, peer_u, peer_v):
    prm = dict(w_a_in=w_a_in, w_a_out=w_a_out, w_b_in=w_b_in, b_conv_w=b_conv_w, b_a_log=b_a_log,
               b_dt_bias=b_dt_bias, b_norm_w=b_norm_w, w_b_out=w_b_out, w_c_in=w_c_in, c_q_norm=c_q_norm,
               c_kv_norm=c_kv_norm, w_c_uq=w_c_uq, w_c_uk=w_c_uk, w_c_uv=w_c_uv, w_c_out=w_c_out,
               ln1_g=ln1_g, ln1_b=ln1_b, ln2_g=ln2_g, ln2_b=ln2_b)
    peer = [(peer_wq[i].astype(BF16), peer_keys[i].astype(BF16), peer_u[i].astype(BF16),
             peer_v[i].astype(BF16).T) for i in range(DEPTH)]
    y_p, a_p, b_conv_p, b_rec_p, c_lat_p = _run_trunk(x_prompt, jnp.arange(x_prompt.shape[1]), prm, peer, None)
    past_len = page_table.shape[1] * cache_c_latent.shape[2]
    cache = dict(a_win=(cache_a_win0, cache_a_win1, cache_a_win2), b_conv=state_b_conv, b_rec=state_b_rec,
                 c_pool=cache_c_latent, page_table=page_table)
    y_s, a_s, b_conv_s, b_rec_s, c_lat_s = _run_trunk(
        x_sample, past_len + jnp.arange(x_sample.shape[1]), prm, peer, cache)
    return (y_p, y_s, a_p[0], a_p[1], a_p[2], a_s[0], a_s[1], a_s[2],
            b_conv_p, b_rec_p, b_conv_s, b_rec_s, c_lat_p, c_lat_s)
```

```python
import functools
import math

import numpy as np
import jax
import jax.numpy as jnp
from jax import lax
from jax.experimental import pallas as pl
from jax.experimental.pallas import tpu as pltpu

F32 = jnp.float32
BF16 = jnp.bfloat16
HI = lax.Precision.HIGHEST

DEPTH = 4
N_MIXERS = 3
DN_ALPHA = (2 * DEPTH) ** 0.25
LN_EPS = 1e-5
RMS_EPS = 1e-6

A_WINDOWS = (128, 512, 2048)
A_DILATIONS = (1, 4, 16)
A_GROUPS = 3
A_HEADS = 8
A_HEAD_DIM = 64
A_SPAN = 128
A_GW = 3 * A_HEADS * A_HEAD_DIM
A_HW = A_HEADS * A_HEAD_DIM
A_UNROLL = 4

B_HEADS = 8
B_DK = 128
B_DV = 128
B_WIDTH = B_HEADS * B_DK
CONV_W = 4
B_CHUNK = 64

C_HEADS = 8
C_NOPE = 128
C_ROPE = 64
C_V = 128
C_KV_LORA = 256
C_Q_LORA = 384
C_LAT = C_KV_LORA + C_ROPE
C_SCALE = (C_NOPE + C_ROPE) ** -0.5
ROPE_THETA = 10000.0

P_HEADS = 8
N_KEYS = 128
P_QDIM = 256
P_TOPK = 16

LANES = 128
VMEM_LIMIT = 56 * 1024 * 1024
NEG_BIG = -1e30
LOG2E = 1.4426950408889634


def _cparams(*sem):
    return pltpu.CompilerParams(dimension_semantics=sem, vmem_limit_bytes=VMEM_LIMIT)


def _round_up(n, m):
    return -(-n // m) * m


def _row_tile(m):
    for t in (512, 256, 128, 64, 32, 16, 8):
        if m % t == 0:
            return t
    raise ValueError(m)


def _layer_norm(v, g, b):
    mu = jnp.mean(v, -1, keepdims=True)
    d = v - mu
    var = jnp.mean(d * d, -1, keepdims=True)
    return d * lax.rsqrt(var + LN_EPS) * g + b


def _bdot(a, b):
    return jnp.dot(a.astype(BF16), b.astype(BF16), preferred_element_type=F32)


def _bdot_nt(a, b):
    return lax.dot_general(a.astype(BF16), b.astype(BF16), (((1,), (1,)), ((), ())),
                           preferred_element_type=F32)


def _fdot(a, b):
    return jnp.dot(a, b, precision=HI, preferred_element_type=F32)


def _softplus(v):
    return jnp.maximum(v, 0.0) + jnp.log1p(jnp.exp(-jnp.abs(v)))


def _mm_kernel(x_ref, w_ref, o_ref):
    o_ref[...] = _bdot(x_ref[...], w_ref[...])


def matmul(x, w):
    m, k = x.shape
    n = w.shape[1]
    tm = min(_row_tile(m), 256)
    return pl.pallas_call(
        _mm_kernel,
        grid=(m // tm,),
        in_specs=[pl.BlockSpec((tm, k), lambda i: (i, 0)),
                  pl.BlockSpec((k, n), lambda i: (0, 0))],
        out_specs=pl.BlockSpec((tm, n), lambda i: (i, 0)),
        out_shape=jax.ShapeDtypeStruct((m, n), F32),
        compiler_params=_cparams("parallel"),
        name="matmul",
    )(x, w)


def _mm_res_ln_kernel(h_ref, w_ref, x_ref, g_ref, b_ref, o_ref):
    y = _bdot(h_ref[...], w_ref[...])
    o_ref[...] = _layer_norm(DN_ALPHA * x_ref[...] + y, g_ref[...], b_ref[...])


def matmul_res_ln(h, w, x, g, b):
    m, k = h.shape
    n = w.shape[1]
    tm = min(_row_tile(m), 256)
    return pl.pallas_call(
        _mm_res_ln_kernel,
        grid=(m // tm,),
        in_specs=[pl.BlockSpec((tm, k), lambda i: (i, 0)),
                  pl.BlockSpec((k, n), lambda i: (0, 0)),
                  pl.BlockSpec((tm, n), lambda i: (i, 0)),
                  pl.BlockSpec((1, n), lambda i: (0, 0)),
                  pl.BlockSpec((1, n), lambda i: (0, 0))],
        out_specs=pl.BlockSpec((tm, n), lambda i: (i, 0)),
        out_shape=jax.ShapeDtypeStruct((m, n), F32),
        compiler_params=_cparams("parallel"),
        name="matmul_res_ln",
    )(h, w, x, g.reshape(1, n), b.reshape(1, n))


def _pair_attention(items):
    lane = lax.broadcasted_iota(jnp.int32, (A_SPAN, LANES), 1)
    low = lane < A_HEAD_DIM
    sels = (low, jnp.logical_not(low))
    scores = [[jnp.where(valid, _bdot_nt(jnp.where(sel, q, 0.0), k) * (A_HEAD_DIM ** -0.5), -jnp.inf)
               for sel in sels] for q, k, _, valid in items]
    probs, lses = [], []
    for pair in scores:
        ps, ls = [], []
        for s in pair:
            mx = jnp.max(s, -1, keepdims=True)
            e = jnp.exp(s - mx)
            den = jnp.sum(e, -1, keepdims=True)
            ps.append(e * (1.0 / den))
            ls.append(jnp.broadcast_to(mx + jnp.log(den), (A_SPAN, LANES)))
        probs.append(ps)
        lses.append(ls)
    outs = [[_bdot(p, item[2]) for p in ps] for ps, item in zip(probs, items)]
    return [(jnp.where(low, o[0], o[1]), jnp.where(low, l[0], l[1])) for o, l in zip(outs, lses)]


def _attn_prompt_kernel(*refs, t):
    qkv = refs[0:3 * A_GROUPS]
    out_ref, o_scr, l_scr = refs[3 * A_GROUPS:]
    qi = lax.broadcasted_iota(jnp.int32, (A_SPAN, 2 * A_SPAN), 0) + A_SPAN
    ki = lax.broadcasted_iota(jnp.int32, (A_SPAN, 2 * A_SPAN), 1)
    band = (qi - ki >= 0) & (qi - ki <= A_SPAN)
    causal = band[:, A_SPAN:]
    for g in range(A_GROUPS):
        q_ref, k_ref, v_ref = qkv[3 * g:3 * g + 3]
        dil = A_DILATIONS[g]
        nb = t // dil // A_SPAN
        if dil == 1:
            def body(n4, carry):
                items, dst = [], []
                for u in range(A_UNROLL):
                    n = n4 * A_UNROLL + u
                    row0 = pl.multiple_of(n * A_SPAN, A_SPAN)
                    prev0 = pl.multiple_of(jnp.maximum(n - 1, 0) * A_SPAN, A_SPAN)
                    kk = jnp.concatenate([k_ref[pl.ds(prev0, A_SPAN), :], k_ref[pl.ds(row0, A_SPAN), :]], axis=0)
                    vv = jnp.concatenate([v_ref[pl.ds(prev0, A_SPAN), :], v_ref[pl.ds(row0, A_SPAN), :]], axis=0)
                    valid = band & ((ki >= A_SPAN) | (n > 0))
                    items.append((q_ref[pl.ds(row0, A_SPAN), :], kk.astype(BF16), vv.astype(BF16), valid))
                    dst.append(pl.ds(row0, A_SPAN))
                for rows, (o, l) in zip(dst, _pair_attention(items)):
                    o_scr[g, rows, :] = o
                    l_scr[g, rows, :] = l
                return carry
            lax.fori_loop(0, nb // A_UNROLL, body, 0)
            continue
        units = [(r, n) for r in range(dil) for n in range(nb)]
        for u0 in range(0, len(units), A_UNROLL):
            items, dst = [], []
            for r, n in units[u0:u0 + A_UNROLL]:
                cur = pl.ds(n * A_SPAN * dil + r, A_SPAN, stride=dil)
                kk, vv, valid = k_ref[cur, :], v_ref[cur, :], causal
                if n > 0:
                    prev = pl.ds((n - 1) * A_SPAN * dil + r, A_SPAN, stride=dil)
                    kk = jnp.concatenate([k_ref[prev, :], kk], axis=0)
                    vv = jnp.concatenate([v_ref[prev, :], vv], axis=0)
                    valid = band
                items.append((q_ref[cur, :], kk.astype(BF16), vv.astype(BF16), valid))
                dst.append(cur)
            for rows, (o, l) in zip(dst, _pair_attention(items)):
                o_scr[g, rows, :] = o
                l_scr[g, rows, :] = l
    chunk = 2 * A_SPAN
    for c0 in range(0, t, chunk):
        rows = slice(c0, c0 + chunk)
        ls = [l_scr[g, rows, :] for g in range(A_GROUPS)]
        mx = jnp.maximum(jnp.maximum(ls[0], ls[1]), ls[2])
        es = [jnp.exp(l - mx) for l in ls]
        inv = 1.0 / (es[0] + es[1] + es[2])
        out_ref[rows, :] = ((es[0] * inv) * o_scr[0, rows, :] + (es[1] * inv) * o_scr[1, rows, :]
                            + (es[2] * inv) * o_scr[2, rows, :])


def attn_prompt(proj):
    b, t, w = proj.shape
    assert t % (A_SPAN * max(A_DILATIONS)) == 0
    npair = A_HW // LANES
    specs = []
    for g in range(A_GROUPS):
        for c in range(3):
            col = (g * A_GW + c * A_HW) // LANES
            specs.append(pl.BlockSpec((None, t, LANES), functools.partial(
                lambda bi, p, col: (bi, 0, col + p), col=col)))
    kern = functools.partial(_attn_prompt_kernel, t=t)
    o = pl.pallas_call(
        kern,
        grid=(b, npair),
        in_specs=specs,
        out_specs=pl.BlockSpec((None, t, LANES), lambda bi, p: (bi, 0, p)),
        out_shape=jax.ShapeDtypeStruct((b, t, A_HW), F32),
        scratch_shapes=[pltpu.VMEM((A_GROUPS, t, LANES), F32), pltpu.VMEM((A_GROUPS, t, LANES), F32)],
        compiler_params=_cparams("parallel", "parallel"),
        name="attn_prompt",
    )(*([proj] * (3 * A_GROUPS)))
    return o.reshape(b * t, A_HW)


A_TP = 8


def _attn_step_kernel(qkv_ref, c0_ref, c1_ref, c2_ref, o_ref, l_ref, *, ts):
    caches = (c0_ref, c1_ref, c2_ref)
    scale = A_HEAD_DIM ** -0.5
    trow = lax.broadcasted_iota(jnp.int32, (A_TP, 1), 0)
    for g in range(A_GROUPS):
        dil = A_DILATIONS[g]
        lb = caches[g].shape[-1]
        tq = lax.broadcasted_iota(jnp.int32, (A_TP, lb), 0)
        pos = lax.broadcasted_iota(jnp.int32, (A_TP, lb), 1)
        seen = ((pos >= tq) if dil == 1 else (pos % dil == tq)) | (tq >= ts)
        for h in range(A_HEADS):
            q = qkv_ref[3 * g, h]
            knew = qkv_ref[3 * g + 1, h]
            vnew = qkv_ref[3 * g + 2, h]
            s = jnp.where(seen, _bdot(q, caches[g][0, h]) * scale, -jnp.inf)
            s_new = []
            for tn in range(ts):
                sn = jnp.sum(q * knew[tn:tn + 1, :], -1, keepdims=True) * scale
                vis = (trow >= tn) if dil == 1 else (trow == tn)
                s_new.append(jnp.where(vis, sn, -jnp.inf))
            mx = jnp.max(s, -1, keepdims=True)
            for sn in s_new:
                mx = jnp.maximum(mx, sn)
            e = jnp.exp(s - mx)
            den = jnp.sum(e, -1, keepdims=True)
            acc = _bdot_nt(e, caches[g][1, h])
            for tn, sn in enumerate(s_new):
                en = jnp.exp(sn - mx)
                den = den + en
                acc = acc + en * vnew[tn:tn + 1, :]
            o_ref[g, h] = acc * (1.0 / den)
            l_ref[g, h] = jnp.broadcast_to(mx + jnp.log(den), (A_TP, A_HEAD_DIM))


def attn_step(qkv, caches, li, ts):
    b = qkv.shape[0]
    views, specs = [], []
    for g in range(A_GROUPS):
        dil = A_DILATIONS[g]
        lb = caches[g].shape[2]
        assert lb == A_SPAN * dil and (dil == 1 or ts <= dil)
        views.append(caches[g].transpose(0, 1, 3, 4, 5, 2))
        specs.append(pl.BlockSpec((None, None, 2, A_HEADS, A_HEAD_DIM, lb), lambda bi: (li, bi, 0, 0, 0, 0)))
    kern = functools.partial(_attn_step_kernel, ts=ts)
    oshape = (b, A_GROUPS, A_HEADS, A_TP, A_HEAD_DIM)
    ospec = pl.BlockSpec((None, A_GROUPS, A_HEADS, A_TP, A_HEAD_DIM), lambda bi: (bi, 0, 0, 0, 0))
    return pl.pallas_call(
        kern,
        grid=(b,),
        in_specs=[pl.BlockSpec((None, 3 * A_GROUPS, A_HEADS, A_TP, A_HEAD_DIM), lambda bi: (bi, 0, 0, 0, 0))] + specs,
        out_specs=[ospec, ospec],
        out_shape=[jax.ShapeDtypeStruct(oshape, F32)] * 2,
        compiler_params=_cparams("parallel"),
        name="attn_step",
    )(qkv, *views)


def _a_out_kernel(o0, o1, o2, l0, l1, l2, w_ref, x_ref, g_ref, b_ref, out_ref):
    ls = (l0[...], l1[...], l2[...])
    mx = jnp.maximum(jnp.maximum(ls[0], ls[1]), ls[2])
    es = [jnp.exp(l - mx) for l in ls]
    inv = 1.0 / (es[0] + es[1] + es[2])
    o = (es[0] * inv) * o0[...] + (es[1] * inv) * o1[...] + (es[2] * inv) * o2[...]
    y = _bdot(o, w_ref[...])
    out_ref[...] = _layer_norm(DN_ALPHA * x_ref[...] + y, g_ref[...], b_ref[...])


def a_out(os_, ls_, w, x, g, b):
    m, d = x.shape
    tm = min(_row_tile(m), 256)
    ol = [os_] * 3 + [ls_] * 3
    ospecs = [pl.BlockSpec((tm, A_HW), functools.partial(lambda i, gg: (i, gg), gg=gg))
              for gg in range(3)] * 2
    return pl.pallas_call(
        _a_out_kernel,
        grid=(m // tm,),
        in_specs=ospecs + [pl.BlockSpec((A_HW, d), lambda i: (0, 0)),
                           pl.BlockSpec((tm, d), lambda i: (i, 0)),
                           pl.BlockSpec((1, d), lambda i: (0, 0)),
                           pl.BlockSpec((1, d), lambda i: (0, 0))],
        out_specs=pl.BlockSpec((tm, d), lambda i: (i, 0)),
        out_shape=jax.ShapeDtypeStruct((m, d), F32),
        compiler_params=_cparams("parallel"),
        name="a_out",
    )(*ol, w, x, g.reshape(1, d), b.reshape(1, d))


def _split_bf16(a):
    hi = a.astype(BF16)
    lo = (a - hi.astype(F32)).astype(BF16)
    return hi, lo


def _dot3(a, b):
    ah, al = _split_bf16(a)
    bh, bl = _split_bf16(b)
    d = functools.partial(jnp.dot, preferred_element_type=F32)
    return d(ah, bh) + (d(ah, bl) + d(al, bh))


def _gdn_kernel(proj_ref, abt_ref, cbuf_ref, s0_ref, convw_ref, alog_row_ref, dt_row_ref, alog_col_ref,
                dt_col_ref, normw_ref, o_ref, s_ref, xp_ref, *, c, t_valid):
    ci = pl.program_id(1)
    qkv_w = 3 * B_WIDTH

    @pl.when(ci == 0)
    def _():
        s_ref[...] = s0_ref[...]
        xp_ref[8 - (CONV_W - 1):8, :] = cbuf_ref[0]

    xp_ref[8:8 + c, :] = proj_ref[0, :, 0:qkv_w]
    y = None
    for j in range(CONV_W):
        lo = 8 - (CONV_W - 1) + j
        term = xp_ref[lo:lo + c, :] * convw_ref[j:j + 1, :]
        y = term if y is None else y + term
    xp_ref[8 - (CONV_W - 1):8, :] = xp_ref[8 + c - (CONV_W - 1):8 + c, :]
    qkv = y * jax.nn.sigmoid(y)

    ab = proj_ref[0, :, 4 * B_WIDTH:4 * B_WIDTH + LANES]
    trow = lax.broadcasted_iota(jnp.int32, (c, LANES), 0) + ci * c
    live = trow < t_valid
    beta = jnp.where(live, jax.nn.sigmoid(ab), 0.0)
    gate = jnp.where(live, -jnp.exp(alog_row_ref[...]) * _softplus(ab + dt_row_ref[...]), 0.0)
    abt = abt_ref[0, 0]
    tcol = lax.broadcasted_iota(jnp.int32, (B_HEADS, c), 1) + ci * c
    gate_t = jnp.where(tcol < t_valid,
                       -jnp.exp(alog_col_ref[...]) * _softplus(abt[B_HEADS:2 * B_HEADS] + dt_col_ref[...]), 0.0)
    ii = lax.broadcasted_iota(jnp.int32, (c, c), 0)
    jj = lax.broadcasted_iota(jnp.int32, (c, c), 1)
    tri = ii >= jj
    stri = ii > jj
    gcum = _fdot(tri.astype(F32), gate)
    gcum_t = _fdot(gate_t, (ii <= jj).astype(F32))
    eye = (ii == jj).astype(F32)

    heads = range(B_HEADS)
    q_, k_, kb_, vb_, gc_, glast_, a_, qk_ = [], [], [], [], [], [], [], []
    for h in heads:
        qh = qkv[:, h * B_DK:(h + 1) * B_DK]
        kh = qkv[:, B_WIDTH + h * B_DK:B_WIDTH + (h + 1) * B_DK]
        vh = qkv[:, 2 * B_WIDTH + h * B_DV:2 * B_WIDTH + (h + 1) * B_DV]
        qh = qh * lax.rsqrt(jnp.sum(qh * qh, -1, keepdims=True) + 1e-6) * (B_DK ** -0.5)
        kh = kh * lax.rsqrt(jnp.sum(kh * kh, -1, keepdims=True) + 1e-6)
        bh = beta[:, h:h + 1]
        q_.append(qh)
        k_.append(kh)
        kb_.append(kh * bh)
        vb_.append(vh * bh)
        gc_.append(gcum[:, B_HEADS + h:B_HEADS + h + 1])
        glast_.append(gcum[c - 1:c, B_HEADS + h:B_HEADS + h + 1])
    kq_ = [_bdot_nt(jnp.concatenate([kb_[h], q_[h]], axis=0), k_[h]) for h in heads]
    for h in heads:
        dmat = jnp.exp(jnp.where(tri, gc_[h] - gcum_t[h:h + 1, :], -jnp.inf))
        a_.append(jnp.where(stri, kq_[h][0:c] * dmat, 0.0))
        qk_.append(jnp.where(tri, kq_[h][c:2 * c] * dmat, 0.0))
    mpow_ = [-a_[h] for h in heads]
    tinv_ = [eye + mpow_[h] for h in heads]
    for _ in range(int(math.log2(c)) - 1):
        mpow_ = [_dot3(mpow_[h], mpow_[h]) for h in heads]
        tinv_ = [tinv_[h] + _dot3(tinv_[h], mpow_[h]) for h in heads]
    egc_ = [jnp.exp(gc_[h]) for h in heads]
    wu_ = [_bdot(tinv_[h], jnp.concatenate([kb_[h] * egc_[h], vb_[h]], axis=1)) for h in heads]
    s_ = [s_ref[0, h] for h in heads]
    ws_ = [_bdot(jnp.concatenate([wu_[h][:, 0:B_DK], q_[h] * egc_[h]], axis=0), s_[h]) for h in heads]
    vnew_ = [wu_[h][:, B_DK:B_DK + B_DV] - ws_[h][0:c] for h in heads]
    o_ = [ws_[h][c:2 * c] + _bdot(qk_[h], vnew_[h]) for h in heads]
    for h in heads:
        kdec = k_[h] * jnp.exp(glast_[h] - gc_[h])
        upd = lax.dot_general(kdec.astype(BF16), vnew_[h].astype(BF16), (((0,), (0,)), ((), ())),
                              preferred_element_type=F32)
        s_ref[0, h] = s_[h] * jnp.exp(glast_[h]) + upd
    for h in heads:
        zh = proj_ref[0, :, qkv_w + h * B_DV:qkv_w + (h + 1) * B_DV]
        on = o_[h] * lax.rsqrt(jnp.mean(o_[h] * o_[h], -1, keepdims=True) + RMS_EPS) * normw_ref[...]
        o_ref[0, :, h * B_DV:(h + 1) * B_DV] = on * (zh * jax.nn.sigmoid(zh))


def gdn(proj, conv_buf, s0, conv_w, a_log, dt_bias, norm_w, t_valid):
    b, t, w = proj.shape
    c = min(B_CHUNK, t)
    n = t // c
    ab = proj[:, :, 4 * B_WIDTH:4 * B_WIDTH + 2 * B_HEADS]
    abt = ab.reshape(b, n, c, 2 * B_HEADS).transpose(0, 1, 3, 2)
    lane_pad = LANES - 2 * B_HEADS
    alog_row = jnp.pad(a_log.reshape(1, B_HEADS), ((0, 0), (B_HEADS, lane_pad)))
    dt_row = jnp.pad(dt_bias.reshape(1, B_HEADS), ((0, 0), (B_HEADS, lane_pad)))
    kern = functools.partial(_gdn_kernel, c=c, t_valid=t_valid)
    const2 = lambda bi, ci: (0, 0)
    return pl.pallas_call(
        kern,
        grid=(b, n),
        in_specs=[pl.BlockSpec((1, c, w), lambda bi, ci: (bi, ci, 0)),
                  pl.BlockSpec((1, 1, 2 * B_HEADS, c), lambda bi, ci: (bi, ci, 0, 0)),
                  pl.BlockSpec((1, CONV_W - 1, 3 * B_WIDTH), lambda bi, ci: (bi, 0, 0)),
                  pl.BlockSpec((1, B_HEADS, B_DK, B_DV), lambda bi, ci: (bi, 0, 0, 0)),
                  pl.BlockSpec((CONV_W, 3 * B_WIDTH), const2),
                  pl.BlockSpec((1, LANES), const2),
                  pl.BlockSpec((1, LANES), const2),
                  pl.BlockSpec((B_HEADS, 1), const2),
                  pl.BlockSpec((B_HEADS, 1), const2),
                  pl.BlockSpec((1, B_DV), const2)],
        out_specs=[pl.BlockSpec((1, c, B_WIDTH), lambda bi, ci: (bi, ci, 0)),
                   pl.BlockSpec((1, B_HEADS, B_DK, B_DV), lambda bi, ci: (bi, 0, 0, 0))],
        out_shape=[jax.ShapeDtypeStruct((b, t, B_WIDTH), F32),
                   jax.ShapeDtypeStruct((b, B_HEADS, B_DK, B_DV), F32)],
        scratch_shapes=[pltpu.VMEM((8 + c, 3 * B_WIDTH), F32)],
        compiler_params=_cparams("parallel", "arbitrary"),
        name="gdn",
    )(proj, abt, conv_buf, s0, conv_w, alog_row, dt_row, a_log.reshape(B_HEADS, 1),
      dt_bias.reshape(B_HEADS, 1), norm_w.reshape(1, B_DV))


def _rot_half(x, width):
    lane = lax.broadcasted_iota(jnp.int32, x.shape, 1)
    first = (lane % C_ROPE) < C_ROPE // 2
    return jnp.where(first, -pltpu.roll(x, width - C_ROPE // 2, 1), pltpu.roll(x, C_ROPE // 2, 1))


def _mla_prep_kernel(proj_ref, cs_ref, qn_ref, kvn_ref, wuq_ref, wuk_ref, lat_ref, qcat_ref):
    rw = C_HEADS * C_ROPE
    cos_all = cs_ref[:, 0:rw]
    sin_all = cs_ref[:, rw:2 * rw]
    pc = proj_ref[...]
    cq = pc[:, 0:C_Q_LORA]
    cq = cq * lax.rsqrt(jnp.mean(cq * cq, -1, keepdims=True) + RMS_EPS) * qn_ref[...]
    ckv = pc[:, C_Q_LORA:C_Q_LORA + C_KV_LORA]
    ckv = ckv * lax.rsqrt(jnp.mean(ckv * ckv, -1, keepdims=True) + RMS_EPS) * kvn_ref[...]
    lat_ref[:, 0:C_KV_LORA] = ckv
    kr = pc[:, C_Q_LORA + C_KV_LORA:C_Q_LORA + C_KV_LORA + LANES]
    kr = kr * cos_all[:, 0:LANES] + _rot_half(kr, LANES) * sin_all[:, 0:LANES]
    lat_ref[:, C_KV_LORA:C_LAT] = kr[:, 0:C_ROPE]
    q = _bdot(cq, wuq_ref[...])
    nw = C_HEADS * C_NOPE
    qr = q[:, nw:nw + rw]
    qr = qr * cos_all + _rot_half(qr, rw) * sin_all
    for h in range(C_HEADS):
        qcat_ref[h, :, 0:C_KV_LORA] = _bdot(q[:, h * C_NOPE:(h + 1) * C_NOPE], wuk_ref[h])
        qcat_ref[h, :, C_KV_LORA:C_LAT] = qr[:, h * C_ROPE:(h + 1) * C_ROPE]


def mla_prep(proj, cs, q_norm, kv_norm, w_uq_perm, w_uk_t):
    m, w = proj.shape
    tm = min(_row_tile(m), 256)
    nrep = cs.shape[0] // tm
    const2 = lambda i: (0, 0)
    return pl.pallas_call(
        _mla_prep_kernel,
        grid=(m // tm,),
        in_specs=[pl.BlockSpec((tm, w), lambda i: (i, 0)),
                  pl.BlockSpec((tm, cs.shape[1]), lambda i: (i % nrep, 0)),
                  pl.BlockSpec((1, C_Q_LORA), const2),
                  pl.BlockSpec((1, C_KV_LORA), const2),
                  pl.BlockSpec(w_uq_perm.shape, const2),
                  pl.BlockSpec(w_uk_t.shape, lambda i: (0, 0, 0))],
        out_specs=[pl.BlockSpec((tm, C_LAT), lambda i: (i, 0)),
                   pl.BlockSpec((C_HEADS, tm, C_LAT), lambda i: (0, i, 0))],
        out_shape=[jax.ShapeDtypeStruct((m, C_LAT), F32),
                   jax.ShapeDtypeStruct((C_HEADS, m, C_LAT), F32)],
        compiler_params=_cparams("parallel"),
        name="mla_prep",
    )(proj, cs, q_norm.reshape(1, -1), kv_norm.reshape(1, -1), w_uq_perm, w_uk_t)


def _mla_flash_kernel(q_ref, lat_ref, wuv_ref, o_ref, m_ref, l_ref, acc_ref, *, tq, tk):
    qi = pl.program_id(1)
    ki = pl.program_id(2)
    last = (qi * tq + tq - 1) // tk
    rows = C_HEADS * tq

    @pl.when(ki == 0)
    def _():
        m_ref[...] = jnp.full_like(m_ref, -jnp.inf)
        l_ref[...] = jnp.zeros_like(l_ref)
        acc_ref[...] = jnp.zeros_like(acc_ref)

    @pl.when(ki <= last)
    def _():
        lat = lat_ref[0].astype(BF16)
        qpos = qi * tq + lax.broadcasted_iota(jnp.int32, (tq, tk), 0)
        kpos = ki * tk + lax.broadcasted_iota(jnp.int32, (tq, tk), 1)
        visible = (kpos <= qpos)[None]
        ngrp = 2
        hg = C_HEADS // ngrp
        gr = [slice(i * hg * tq, (i + 1) * hg * tq) for i in range(ngrp)]
        ss = [_bdot_nt(q_ref[i * hg:(i + 1) * hg].reshape(hg * tq, C_LAT), lat) * C_SCALE for i in range(ngrp)]
        ss = [jnp.where(visible, s.reshape(hg, tq, tk), -jnp.inf).reshape(hg * tq, tk) for s in ss]
        ps, alphas = [], []
        for i, s in enumerate(ss):
            m_old = m_ref[gr[i], :]
            m_new = jnp.maximum(m_old, jnp.max(s, -1, keepdims=True))
            alpha = jnp.exp(m_old - m_new)
            p = jnp.exp(s - m_new)
            l_ref[gr[i], :] = alpha * l_ref[gr[i], :] + jnp.sum(p, -1, keepdims=True)
            m_ref[gr[i], :] = m_new
            ps.append(p)
            alphas.append(alpha)
        for i in range(ngrp):
            acc_ref[gr[i], :] = alphas[i] * acc_ref[gr[i], :] + _bdot(ps[i], lat[:, 0:C_KV_LORA])

    @pl.when(ki == last)
    def _():
        o = acc_ref[...] * (1.0 / l_ref[...])
        for h in range(C_HEADS):
            o_ref[:, h * C_V:(h + 1) * C_V] = _bdot(o[h * tq:(h + 1) * tq], wuv_ref[h])


def mla_flash(qcat, lat, w_uv_t, b, t):
    tq, tk = 128, 512
    tk = min(tk, t)
    tq = min(tq, t)
    nq, nk = t // tq, t // tk
    kern = functools.partial(_mla_flash_kernel, tq=tq, tk=tk)
    rows = C_HEADS * tq
    return pl.pallas_call(
        kern,
        grid=(b, nq, nk),
        in_specs=[pl.BlockSpec((C_HEADS, tq, C_LAT), lambda bi, qi, ki: (0, bi * nq + qi, 0)),
                  pl.BlockSpec((1, tk, C_LAT),
                               lambda bi, qi, ki: (bi, jnp.minimum(ki, (qi * tq + tq - 1) // tk), 0)),
                  pl.BlockSpec(w_uv_t.shape, lambda bi, qi, ki: (0, 0, 0))],
        out_specs=pl.BlockSpec((tq, C_HEADS * C_V), lambda bi, qi, ki: (bi * nq + qi, 0)),
        out_shape=jax.ShapeDtypeStruct((b * t, C_HEADS * C_V), F32),
        scratch_shapes=[pltpu.VMEM((rows, 1), F32), pltpu.VMEM((rows, 1), F32),
                        pltpu.VMEM((rows, C_KV_LORA), F32)],
        compiler_params=_cparams("parallel", "parallel", "arbitrary"),
        name="mla_flash",
    )(qcat, lat, w_uv_t)


def _mla_step_kernel(pt_ref, q_ref, new_ref, *rest, ts, npg):
    page_refs = rest[:npg]
    wuv_ref, o_ref, m_ref, l_ref, acc_ref = rest[npg:]
    si = pl.program_id(1)
    rows = ts * C_HEADS
    q = q_ref[0]

    @pl.when(si == 0)
    def _():
        new = new_ref[0]
        tq_ = lax.broadcasted_iota(jnp.int32, (rows, 1), 0) // C_HEADS
        s_new = []
        for tn in range(ts):
            sn = jnp.sum(q * new[tn:tn + 1, :], -1, keepdims=True) * C_SCALE
            s_new.append(jnp.where(tq_ >= tn, sn, -jnp.inf))
        mx = s_new[0]
        for sn in s_new[1:]:
            mx = jnp.maximum(mx, sn)
        den = jnp.zeros((rows, 1), F32)
        acc = jnp.zeros((rows, C_KV_LORA), F32)
        for tn in range(ts):
            e = jnp.exp(s_new[tn] - mx)
            den = den + e
            acc = acc + e * new[tn:tn + 1, 0:C_KV_LORA]
        m_ref[...] = mx
        l_ref[...] = den
        acc_ref[...] = acc

    s = jnp.concatenate([_bdot(q, pr[0]) for pr in page_refs], axis=1) * C_SCALE
    m_old = m_ref[...]
    m_new = jnp.maximum(m_old, jnp.max(s, -1, keepdims=True))
    alpha = jnp.exp(m_old - m_new)
    p = jnp.exp(s - m_new)
    l_ref[...] = alpha * l_ref[...] + jnp.sum(p, -1, keepdims=True)
    acc = alpha * acc_ref[...]
    psz = page_refs[0].shape[2]
    for i, pr in enumerate(page_refs):
        acc = acc + _bdot_nt(p[:, i * psz:(i + 1) * psz], pr[0, 0:C_KV_LORA, :])
    acc_ref[...] = acc
    m_ref[...] = m_new

    @pl.when(si == pl.num_programs(1) - 1)
    def _():
        o = acc_ref[...] * (1.0 / l_ref[...])
        full = _bdot(o, wuv_ref[...])
        hrow = lax.broadcasted_iota(jnp.int32, (C_HEADS, C_HEADS * C_V), 0)
        hcol = lax.broadcasted_iota(jnp.int32, (C_HEADS, C_HEADS * C_V), 1) // C_V
        diag = hrow == hcol
        for t in range(ts):
            blk = full[t * C_HEADS:(t + 1) * C_HEADS]
            o_ref[0, t:t + 1, :] = jnp.sum(jnp.where(diag, blk, 0.0), 0, keepdims=True)


def mla_step(qrows, lat_new, pool, page_table, w_uv_flat):
    b, rows, _ = qrows.shape
    ts = rows // C_HEADS
    n_pages = page_table.shape[1]
    psz = pool.shape[1]
    pool = pool.transpose(0, 2, 1)
    npg = 16
    while n_pages % npg:
        npg //= 2
    kern = functools.partial(_mla_step_kernel, ts=ts, npg=npg)

    def page_map(bi, si, pt, k):
        return (pt[bi, si * npg + k], 0, 0)

    grid_spec = pltpu.PrefetchScalarGridSpec(
        num_scalar_prefetch=1,
        grid=(b, n_pages // npg),
        in_specs=[pl.BlockSpec((1, rows, C_LAT), lambda bi, si, pt: (bi, 0, 0)),
                  pl.BlockSpec((1, ts, C_LAT), lambda bi, si, pt: (bi, 0, 0))]
                 + [pl.BlockSpec((1, C_LAT, psz), functools.partial(page_map, k=k)) for k in range(npg)]
                 + [pl.BlockSpec(w_uv_flat.shape, lambda bi, si, pt: (0, 0))],
        out_specs=pl.BlockSpec((1, ts, C_HEADS * C_V), lambda bi, si, pt: (bi, 0, 0)),
        scratch_shapes=[pltpu.VMEM((rows, 1), F32), pltpu.VMEM((rows, 1), F32),
                        pltpu.VMEM((rows, C_KV_LORA), F32)],
    )
    return pl.pallas_call(
        kern,
        grid_spec=grid_spec,
        out_shape=jax.ShapeDtypeStruct((b, ts, C_HEADS * C_V), F32),
        compiler_params=_cparams("parallel", "arbitrary"),
        name="mla_step",
    )(page_table, qrows, lat_new, *([pool] * npg), w_uv_flat)


P_RANKS = P_TOPK + 1


def _pruned_pairs():
    return [(i, j) for i in range(P_RANKS) for j in range(P_RANKS // (i + 1))]


def _extract_top(s, n):
    vals = []
    for _ in range(n):
        mx = jnp.max(s, 0, keepdims=True)
        vals.append(mx)
        s = jnp.where(s == mx, -jnp.inf, s)
    return vals


def _candidates(avals, bvals, tn):
    pairs = _pruned_pairs()
    sub = lax.broadcasted_iota(jnp.int32, (8, tn), 0)
    groups = []
    for g0 in range(0, len(pairs), 8):
        grp = pairs[g0:g0 + 8]
        av = jnp.broadcast_to(avals[grp[0][0]], (8, tn))
        bv = jnp.broadcast_to(bvals[grp[0][1]], (8, tn))
        for r in range(1, len(grp)):
            if grp[r][0] != grp[r - 1][0]:
                av = jnp.where(sub >= r, avals[grp[r][0]], av)
            if grp[r][1] != grp[r - 1][1]:
                bv = jnp.where(sub >= r, bvals[grp[r][1]], bv)
        cand = av + bv
        if len(grp) < 8:
            cand = jnp.where(sub < len(grp), cand, -jnp.inf)
        groups.append(cand)
    return jnp.concatenate(groups, axis=0)


def _peer_topk_kernel(q_ref, keys_ref, s1_ref, s2_ref, thr_ref):
    tn = q_ref.shape[0]
    half = P_QDIM // 2
    sub = lax.broadcasted_iota(jnp.int32, (P_HEADS, tn), 0)
    thr = jnp.zeros((P_HEADS, tn), F32)
    for h in range(P_HEADS):
        sc = []
        for c in range(2):
            qs = q_ref[:, h * P_QDIM + c * half:h * P_QDIM + (c + 1) * half]
            s = _bdot_nt(keys_ref[h, c], qs)
            sc.append(s - jnp.max(s, 0, keepdims=True))
        a = _extract_top(sc[0], P_RANKS)
        bvals = _extract_top(sc[1], P_RANKS)
        top = _extract_top(_candidates(a, bvals, tn), P_RANKS)
        z = jnp.exp(top[0])
        for v in top[1:P_TOPK]:
            z = z + jnp.exp(v)
        logz = jnp.log(z)
        s1_ref[h] = (sc[0] - logz) * LOG2E
        s2_ref[h] = sc[1] * LOG2E
        mid = 0.5 * (top[P_TOPK - 1] + top[P_TOPK])
        thr = jnp.where(sub == h, (mid - logz) * LOG2E, thr)
    thr_ref[...] = thr


def peer_topk(q, keys_bf16):
    m = q.shape[0]
    tn = LANES
    return pl.pallas_call(
        _peer_topk_kernel,
        grid=(m // tn,),
        in_specs=[pl.BlockSpec((tn, q.shape[1]), lambda i: (i, 0)),
                  pl.BlockSpec(keys_bf16.shape, lambda i: (0, 0, 0, 0))],
        out_specs=[pl.BlockSpec((P_HEADS, N_KEYS, tn), lambda i: (0, 0, i)),
                   pl.BlockSpec((P_HEADS, N_KEYS, tn), lambda i: (0, 0, i)),
                   pl.BlockSpec((P_HEADS, tn), lambda i: (0, i))],
        out_shape=[jax.ShapeDtypeStruct((P_HEADS, N_KEYS, m), F32),
                   jax.ShapeDtypeStruct((P_HEADS, N_KEYS, m), F32),
                   jax.ShapeDtypeStruct((P_HEADS, m), F32)],
        compiler_params=_cparams("parallel"),
        name="peer_topk",
    )(q, keys_bf16)


def _gelu(v):
    return 0.5 * v * (1.0 + lax.erf(v * (2.0 ** -0.5)))


PEER_I1 = 8
PEER_TE = PEER_I1 * N_KEYS
PEER_TN = 256


def _expert_hidden(u_ref, xb_ref, rows=slice(None)):
    return lax.dot_general(u_ref[rows, :], xb_ref[...], (((1,), (1,)), ((), ())), preferred_element_type=F32)


def _gate_stage(s1_ref, s2_ref, thr_ref, ht_ref, at_ref, i1_base, ci, between=()):
    sub = 8
    cols = slice(ci * LANES, (ci + 1) * LANES)
    s1_rows = [s1_ref[h, pl.ds(i1_base, PEER_I1), cols] for h in range(P_HEADS)]
    thr = thr_ref[:, cols]
    thr_b = [jnp.broadcast_to(thr[h:h + 1, :], (sub, LANES)) for h in range(P_HEADS)]
    between = list(between)
    every = PEER_I1 // max(len(between), 1)
    for j in range(PEER_I1):
        if between and j % every == 0:
            between.pop(0)()
        gsum = [jnp.zeros((sub, LANES), F32)] * (N_KEYS // sub)
        for h in range(P_HEADS):
            row = jnp.broadcast_to(s1_rows[h][j:j + 1, :], (sub, LANES))
            for v in range(N_KEYS // sub):
                ssum = row + s2_ref[h, v * sub:(v + 1) * sub, cols]
                gsum[v] = gsum[v] + jnp.exp2(jnp.where(ssum >= thr_b[h], ssum, NEG_BIG))
        rows = slice(j * N_KEYS, (j + 1) * N_KEYS)
        act = _gelu(ht_ref[rows, cols]) * jnp.concatenate(gsum, axis=0)
        at_ref[rows, cols] = act.astype(BF16)


def _peer_dense_kernel(flag_ref, x_ref, s1_ref, s2_ref, thr_ref, u0_ref, ua_ref, ub_ref, va_ref, vb_ref, vl_ref,
                       g_ref, b_ref, o_ref, xb_ref, ht0_ref, ht1_ref, at0_ref, at1_ref, acc_ref):
    si = pl.program_id(1)
    tm = x_ref.shape[0]
    nc = tm // LANES
    vdot = functools.partial(jnp.dot, preferred_element_type=F32)

    @pl.when(si == 0)
    def _():
        xb_ref[...] = x_ref[...].astype(BF16)
        acc_ref[...] = jnp.zeros_like(acc_ref)
        at1_ref[...] = jnp.zeros_like(at1_ref)
        ht0_ref[...] = _expert_hidden(u0_ref, xb_ref)

    te, d = ua_ref.shape

    def tile(u_next_ref, ht_next_ref, v_prev_ref, at_prev_ref, ht_ref, at_ref, i1_base):
        for ci in range(nc):
            er = slice(ci * (te // nc), (ci + 1) * (te // nc))
            dr = slice(ci * (d // nc), (ci + 1) * (d // nc))
            pieces = []
            tn = min(PEER_TN, tm)
            for n0 in range(0, tm, tn):
                tok = slice(n0, n0 + tn)

                def hidden(tok=tok, er=er):
                    ht_next_ref[er, tok] = lax.dot_general(
                        u_next_ref[er, :], xb_ref[tok, :], (((1,), (1,)), ((), ())), preferred_element_type=F32)

                def value(tok=tok, dr=dr):
                    acc_ref[dr, tok] += vdot(v_prev_ref[dr, :], at_prev_ref[:, tok])

                pieces += [hidden, value]

            @pl.when(flag_ref[0] > 0)
            def _(ci=ci, pieces=pieces):
                _gate_stage(s1_ref, s2_ref, thr_ref, ht_ref, at_ref, i1_base, ci, pieces)

    i1_even = pl.multiple_of(si * (2 * PEER_I1), PEER_I1)

    tile(ua_ref, ht1_ref, va_ref, at1_ref, ht0_ref, at0_ref, i1_even)
    tile(ub_ref, ht0_ref, vb_ref, at0_ref, ht1_ref, at1_ref, i1_even + PEER_I1)

    @pl.when(si == pl.num_programs(1) - 1)
    def _():
        y = acc_ref[...] + vdot(vl_ref[...], at1_ref[...])
        o_ref[...] = _layer_norm(DN_ALPHA * x_ref[...] + y.T, g_ref[...], b_ref[...])


def peer_dense(x, s1, s2, thr, u_bf16, vt_bf16, g, b):
    m, d = x.shape
    nt = u_bf16.shape[0] // PEER_TE
    assert nt % 2 == 0
    tm = min(_row_tile(m), 512)
    last = nt - 1
    u_spec = lambda f: pl.BlockSpec((PEER_TE, d), lambda i, s, fl: (f(s), 0))
    v_spec = lambda f: pl.BlockSpec((d, PEER_TE), lambda i, s, fl: (0, f(s)))
    grid_spec = pltpu.PrefetchScalarGridSpec(
        num_scalar_prefetch=1,
        grid=(m // tm, nt // 2),
        in_specs=[pl.BlockSpec((tm, d), lambda i, s, fl: (i, 0)),
                  pl.BlockSpec((P_HEADS, N_KEYS, tm), lambda i, s, fl: (0, 0, i)),
                  pl.BlockSpec((P_HEADS, N_KEYS, tm), lambda i, s, fl: (0, 0, i)),
                  pl.BlockSpec((P_HEADS, tm), lambda i, s, fl: (0, i)),
                  u_spec(lambda s: 0),
                  u_spec(lambda s: 2 * s + 1),
                  u_spec(lambda s: jnp.minimum(2 * s + 2, last)),
                  v_spec(lambda s: jnp.maximum(2 * s - 1, 0)),
                  v_spec(lambda s: 2 * s),
                  v_spec(lambda s: last),
                  pl.BlockSpec((1, d), lambda i, s, fl: (0, 0)),
                  pl.BlockSpec((1, d), lambda i, s, fl: (0, 0))],
        out_specs=pl.BlockSpec((tm, d), lambda i, s, fl: (i, 0)),
        scratch_shapes=[pltpu.VMEM((tm, d), BF16),
                        pltpu.VMEM((PEER_TE, tm), F32), pltpu.VMEM((PEER_TE, tm), F32),
                        pltpu.VMEM((PEER_TE, tm), BF16), pltpu.VMEM((PEER_TE, tm), BF16),
                        pltpu.VMEM((d, tm), F32)])
    return pl.pallas_call(
        _peer_dense_kernel,
        grid_spec=grid_spec,
        out_shape=jax.ShapeDtypeStruct((m, d), F32),
        compiler_params=_cparams("parallel", "arbitrary"),
        name="peer_dense",
    )(jnp.ones((1,), jnp.int32), x, s1, s2, thr, u_bf16, u_bf16, u_bf16, vt_bf16, vt_bf16, vt_bf16,
      g.reshape(1, d), b.reshape(1, d))


def peer_layer(x, wq_bf16, keys_bf16, u_bf16, vt_bf16, g, b):
    q = matmul(x, wq_bf16)
    s1, s2, thr = peer_topk(q, keys_bf16)
    return peer_dense(x, s1, s2, thr, u_bf16, vt_bf16, g, b)


def _pad_cols(w, mult=LANES):
    n = w.shape[-1]
    return jnp.pad(w, ((0, 0),) * (w.ndim - 1) + ((0, _round_up(n, mult) - n),))


def _rope_table(pos):
    half = C_ROPE // 2
    inv = ROPE_THETA ** (-jnp.arange(half, dtype=F32) / half)
    ang = pos.astype(F32)[:, None] * inv[None, :]
    cos = jnp.tile(jnp.cos(ang), (1, 2 * C_HEADS))
    sin = jnp.tile(jnp.sin(ang), (1, 2 * C_HEADS))
    return jnp.concatenate([cos, sin], axis=1)


def _mixer_a(x, bsz, t, w_in, w_out, g, b, bufs, li):
    m, d = x.shape
    proj = matmul(x, w_in.astype(BF16))
    p3 = proj.reshape(bsz, t, A_GROUPS * A_GW)
    rows = []
    if bufs is None:
        for gi in range(A_GROUPS):
            keep = min(A_WINDOWS[gi], t)
            rows.append(p3[:, t - keep:, gi * A_GW + A_HW:(gi + 1) * A_GW]
                        .reshape(bsz, keep, 2, A_HEADS, A_HEAD_DIM))
        x1 = matmul_res_ln(attn_prompt(p3), w_out.astype(BF16), x, g, b)
    else:
        qkv = p3.reshape(bsz, t, 3 * A_GROUPS, A_HEADS, A_HEAD_DIM)
        for gi in range(A_GROUPS):
            rows.append(qkv[:, :, 3 * gi + 1:3 * gi + 3])
        qkv_t = jnp.pad(qkv.transpose(0, 2, 3, 1, 4), ((0, 0), (0, 0), (0, 0), (0, A_TP - t), (0, 0)))
        o, l = attn_step(qkv_t, bufs, li, t)
        tok_major = lambda a: a[:, :, :, :t].transpose(0, 3, 1, 2, 4).reshape(m, A_GROUPS * A_HW)
        x1 = a_out(tok_major(o), tok_major(l), w_out.astype(BF16), x, g, b)
    return x1, rows


def _mixer_b(x, bsz, t, w_in, conv_w, a_log, dt_bias, norm_w, w_out, g, b, conv_buf, s0):
    proj = matmul(x, _pad_cols(w_in).astype(BF16))
    p3 = proj.reshape(bsz, t, proj.shape[1])
    if conv_buf is None:
        conv_buf = jnp.zeros((bsz, CONV_W - 1, 3 * B_WIDTH), F32)
    if s0 is None:
        s0 = jnp.zeros((bsz, B_HEADS, B_DK, B_DV), F32)
    new_buf = jnp.concatenate([conv_buf, p3[:, :, :3 * B_WIDTH]], axis=1)[:, t:]
    c = min(B_CHUNK, _round_up(t, 8))
    tp = _round_up(t, c)
    if tp != t:
        p3 = jnp.pad(p3, ((0, 0), (0, tp - t), (0, 0)))
    o, s_fin = gdn(p3, conv_buf, s0, conv_w, a_log, dt_bias, norm_w, t)
    o = o[:, :t].reshape(bsz * t, B_WIDTH)
    x1 = matmul_res_ln(o, w_out.astype(BF16), x, g, b)
    return x1, new_buf, s_fin


def _mixer_c(x, bsz, t, pos, w_in, q_norm, kv_norm, w_uq, w_uk, w_uv, w_out, g, b, pool, page_table):
    m = x.shape[0]
    proj = matmul(x, _pad_cols(w_in).astype(BF16))
    wq3 = w_uq.reshape(C_Q_LORA, C_HEADS, C_NOPE + C_ROPE)
    w_uq_perm = jnp.concatenate([wq3[:, :, :C_NOPE].reshape(C_Q_LORA, -1),
                                 wq3[:, :, C_NOPE:].reshape(C_Q_LORA, -1)], axis=1).astype(BF16)
    w_uk_t = w_uk.transpose(1, 2, 0).astype(BF16)
    tm = min(_row_tile(m), 256)
    table = _rope_table(pos)
    reps = max(tm // t, 1)
    cs = jnp.tile(table, (reps, 1))
    lat_new, qcat = mla_prep(proj, cs, q_norm, kv_norm, w_uq_perm, w_uk_t)
    if pool is None:
        w_uv_t = w_uv.transpose(1, 0, 2).astype(BF16)
        o = mla_flash(qcat, lat_new.reshape(bsz, t, C_LAT), w_uv_t, bsz, t)
    else:
        qrows = qcat.reshape(C_HEADS, bsz, t, C_LAT).transpose(1, 2, 0, 3).reshape(bsz, t * C_HEADS, C_LAT)
        w_uv_flat = w_uv.reshape(C_KV_LORA, C_HEADS * C_V).astype(BF16)
        o = mla_step(qrows, lat_new.reshape(bsz, t, C_LAT), pool, page_table, w_uv_flat)
        o = o.reshape(m, C_HEADS * C_V)
    x1 = matmul_res_ln(o, w_out.astype(BF16), x, g, b)
    return x1, lat_new.reshape(bsz, t, C_LAT)


def _run_trunk(x3, pos, prm, peer, cache):
    bsz, t, d = x3.shape
    x = x3.reshape(bsz * t, d)
    new_a = [[] for _ in range(A_GROUPS)]
    new_conv, new_rec, new_lat = [], [], []
    for i in range(DEPTH):
        kind, li = i % N_MIXERS, i // N_MIXERS
        g1, b1 = prm['ln1_g'][i], prm['ln1_b'][i]
        if kind == 0:
            bufs = None if cache is None else cache['a_win']
            x, rows = _mixer_a(x, bsz, t, prm['w_a_in'][li], prm['w_a_out'][li], g1, b1, bufs, li)
            for gi in range(A_GROUPS):
                new_a[gi].append(rows[gi])
        elif kind == 1:
            conv_buf = None if cache is None else cache['b_conv'][li]
            s0 = None if cache is None else cache['b_rec'][li]
            x, cb, st = _mixer_b(x, bsz, t, prm['w_b_in'][li], prm['b_conv_w'][li], prm['b_a_log'][li],
                                 prm['b_dt_bias'][li], prm['b_norm_w'][li], prm['w_b_out'][li], g1, b1,
                                 conv_buf, s0)
            new_conv.append(cb)
            new_rec.append(st)
        else:
            pool = None if cache is None else cache['c_pool'][li]
            pt = None if cache is None else cache['page_table']
            x, lat = _mixer_c(x, bsz, t, pos, prm['w_c_in'][li], prm['c_q_norm'][li], prm['c_kv_norm'][li],
                              prm['w_c_uq'][li], prm['w_c_uk'][li], prm['w_c_uv'][li], prm['w_c_out'][li],
                              g1, b1, pool, pt)
            new_lat.append(lat)
        wq, keys, u, vt = peer[i]
        x = peer_layer(x, wq, keys, u, vt, prm['ln2_g'][i], prm['ln2_b'][i])
    return (x.reshape(bsz, t, d), [jnp.stack(r) for r in new_a], jnp.stack(new_conv), jnp.stack(new_rec),
            jnp.stack(new_lat))


def kernel(x_prompt, x_sample, cache_a_win0, cache_a_win1, cache_a_win2, state_b_conv, state_b_rec,
           cache_c_latent, page_table, w_a_in, w_a_out, w_b_in, b_conv_w, b_a_log, b_dt_bias, b_norm_w,
           w_b_out, w_c_in, c_q_norm, c_kv_norm, w_c_uq, w_c_uk, w_c_uv, w_c_out, ln1_g, ln1_b, ln2_g,
           ln2_b, peer_wq, peer_keys, peer_u, peer_v):
    prm = dict(w_a_in=w_a_in, w_a_out=w_a_out, w_b_in=w_b_in, b_conv_w=b_conv_w, b_a_log=b_a_log,
               b_dt_bias=b_dt_bias, b_norm_w=b_norm_w, w_b_out=w_b_out, w_c_in=w_c_in, c_q_norm=c_q_norm,
               c_kv_norm=c_kv_norm, w_c_uq=w_c_uq, w_c_uk=w_c_uk, w_c_uv=w_c_uv, w_c_out=w_c_out,
               ln1_g=ln1_g, ln1_b=ln1_b, ln2_g=ln2_g, ln2_b=ln2_b)
    peer = [(peer_wq[i].astype(BF16), peer_keys[i].astype(BF16), peer_u[i].astype(BF16),
             peer_v[i].astype(BF16).T) for i in range(DEPTH)]
    y_p, a_p, b_conv_p, b_rec_p, c_lat_p = _run_trunk(x_prompt, jnp.arange(x_prompt.shape[1]), prm, peer, None)
    past_len = page_table.shape[1] * cache_c_latent.shape[2]
    cache = dict(a_win=(cache_a_win0, cache_a_win1, cache_a_win2), b_conv=state_b_conv, b_rec=state_b_rec,
                 c_pool=cache_c_latent, page_table=page_table)
    y_s, a_s, b_conv_s, b_rec_s, c_lat_s = _run_trunk(
        x_sample, past_len + jnp.arange(x_sample.shape[1]), prm, peer, cache)
    return (y_p, y_s, a_p[0], a_p[1], a_p[2], a_s[0], a_s[1], a_s[2],
            b_conv_p, b_rec_p, b_conv_s, b_rec_s, c_lat_p, c_lat_s)
```

```python
import functools
import math

import numpy as np
import jax
import jax.numpy as jnp
from jax import lax
from jax.experimental import pallas as pl
from jax.experimental.pallas import tpu as pltpu

F32 = jnp.float32
BF16 = jnp.bfloat16
HI = lax.Precision.HIGHEST

DEPTH = 4
N_MIXERS = 3
DN_ALPHA = (2 * DEPTH) ** 0.25
LN_EPS = 1e-5
RMS_EPS = 1e-6

A_WINDOWS = (128, 512, 2048)
A_DILATIONS = (1, 4, 16)
A_GROUPS = 3
A_HEADS = 8
A_HEAD_DIM = 64
A_SPAN = 128
A_GW = 3 * A_HEADS * A_HEAD_DIM
A_HW = A_HEADS * A_HEAD_DIM
A_UNROLL = 4

B_HEADS = 8
B_DK = 128
B_DV = 128
B_WIDTH = B_HEADS * B_DK
CONV_W = 4
B_CHUNK = 64

C_HEADS = 8
C_NOPE = 128
C_ROPE = 64
C_V = 128
C_KV_LORA = 256
C_Q_LORA = 384
C_LAT = C_KV_LORA + C_ROPE
C_SCALE = (C_NOPE + C_ROPE) ** -0.5
ROPE_THETA = 10000.0

P_HEADS = 8
N_KEYS = 128
P_QDIM = 256
P_TOPK = 16

LANES = 128
VMEM_LIMIT = 56 * 1024 * 1024
NEG_BIG = -1e30
LOG2E = 1.4426950408889634


def _cparams(*sem):
    return pltpu.CompilerParams(dimension_semantics=sem, vmem_limit_bytes=VMEM_LIMIT)


def _round_up(n, m):
    return -(-n // m) * m


def _row_tile(m):
    for t in (512, 256, 128, 64, 32, 16, 8):
        if m % t == 0:
            return t
    raise ValueError(m)


def _layer_norm(v, g, b):
    mu = jnp.mean(v, -1, keepdims=True)
    d = v - mu
    var = jnp.mean(d * d, -1, keepdims=True)
    return d * lax.rsqrt(var + LN_EPS) * g + b


def _bdot(a, b):
    return jnp.dot(a.astype(BF16), b.astype(BF16), preferred_element_type=F32)


def _bdot_nt(a, b):
    return lax.dot_general(a.astype(BF16), b.astype(BF16), (((1,), (1,)), ((), ())),
                           preferred_element_type=F32)


def _fdot(a, b):
    return jnp.dot(a, b, precision=HI, preferred_element_type=F32)


def _softplus(v):
    return jnp.maximum(v, 0.0) + jnp.log1p(jnp.exp(-jnp.abs(v)))


def _mm_kernel(x_ref, w_ref, o_ref):
    o_ref[...] = _bdot(x_ref[...], w_ref[...])


def matmul(x, w):
    m, k = x.shape
    n = w.shape[1]
    tm = min(_row_tile(m), 256)
    return pl.pallas_call(
        _mm_kernel,
        grid=(m // tm,),
        in_specs=[pl.BlockSpec((tm, k), lambda i: (i, 0)),
                  pl.BlockSpec((k, n), lambda i: (0, 0))],
        out_specs=pl.BlockSpec((tm, n), lambda i: (i, 0)),
        out_shape=jax.ShapeDtypeStruct((m, n), F32),
        compiler_params=_cparams("parallel"),
        name="matmul",
    )(x, w)


def _mm_res_ln_kernel(h_ref, w_ref, x_ref, g_ref, b_ref, o_ref):
    y = _bdot(h_ref[...], w_ref[...])
    o_ref[...] = _layer_norm(DN_ALPHA * x_ref[...] + y, g_ref[...], b_ref[...])


def matmul_res_ln(h, w, x, g, b):
    m, k = h.shape
    n = w.shape[1]
    tm = min(_row_tile(m), 256)
    return pl.pallas_call(
        _mm_res_ln_kernel,
        grid=(m // tm,),
        in_specs=[pl.BlockSpec((tm, k), lambda i: (i, 0)),
                  pl.BlockSpec((k, n), lambda i: (0, 0)),
                  pl.BlockSpec((tm, n), lambda i: (i, 0)),
                  pl.BlockSpec((1, n), lambda i: (0, 0)),
                  pl.BlockSpec((1, n), lambda i: (0, 0))],
        out_specs=pl.BlockSpec((tm, n), lambda i: (i, 0)),
        out_shape=jax.ShapeDtypeStruct((m, n), F32),
        compiler_params=_cparams("parallel"),
        name="matmul_res_ln",
    )(h, w, x, g.reshape(1, n), b.reshape(1, n))


def _pair_attention(items):
    lane = lax.broadcasted_iota(jnp.int32, (A_SPAN, LANES), 1)
    low = lane < A_HEAD_DIM
    sels = (low, jnp.logical_not(low))
    scores = [[jnp.where(valid, _bdot_nt(jnp.where(sel, q, 0.0), k) * (A_HEAD_DIM ** -0.5), -jnp.inf)
               for sel in sels] for q, k, _, valid in items]
    probs, lses = [], []
    for pair in scores:
        ps, ls = [], []
        for s in pair:
            mx = jnp.max(s, -1, keepdims=True)
            e = jnp.exp(s - mx)
            den = jnp.sum(e, -1, keepdims=True)
            ps.append(e * (1.0 / den))
            ls.append(jnp.broadcast_to(mx + jnp.log(den), (A_SPAN, LANES)))
        probs.append(ps)
        lses.append(ls)
    outs = [[_bdot(p, item[2]) for p in ps] for ps, item in zip(probs, items)]
    return [(jnp.where(low, o[0], o[1]), jnp.where(low, l[0], l[1])) for o, l in zip(outs, lses)]


def _attn_prompt_kernel(*refs, t):
    qkv = refs[0:3 * A_GROUPS]
    out_ref, o_scr, l_scr = refs[3 * A_GROUPS:]
    qi = lax.broadcasted_iota(jnp.int32, (A_SPAN, 2 * A_SPAN), 0) + A_SPAN
    ki = lax.broadcasted_iota(jnp.int32, (A_SPAN, 2 * A_SPAN), 1)
    band = (qi - ki >= 0) & (qi - ki <= A_SPAN)
    causal = band[:, A_SPAN:]
    for g in range(A_GROUPS):
        q_ref, k_ref, v_ref = qkv[3 * g:3 * g + 3]
        dil = A_DILATIONS[g]
        nb = t // dil // A_SPAN
        if dil == 1:
            def body(n4, carry):
                items, dst = [], []
                for u in range(A_UNROLL):
                    n = n4 * A_UNROLL + u
                    row0 = pl.multiple_of(n * A_SPAN, A_SPAN)
                    prev0 = pl.multiple_of(jnp.maximum(n - 1, 0) * A_SPAN, A_SPAN)
                    kk = jnp.concatenate([k_ref[pl.ds(prev0, A_SPAN), :], k_ref[pl.ds(row0, A_SPAN), :]], axis=0)
                    vv = jnp.concatenate([v_ref[pl.ds(prev0, A_SPAN), :], v_ref[pl.ds(row0, A_SPAN), :]], axis=0)
                    valid = band & ((ki >= A_SPAN) | (n > 0))
                    items.append((q_ref[pl.ds(row0, A_SPAN), :], kk.astype(BF16), vv.astype(BF16), valid))
                    dst.append(pl.ds(row0, A_SPAN))
                for rows, (o, l) in zip(dst, _pair_attention(items)):
                    o_scr[g, rows, :] = o
                    l_scr[g, rows, :] = l
                return carry
            lax.fori_loop(0, nb // A_UNROLL, body, 0)
            continue
        units = [(r, n) for r in range(dil) for n in range(nb)]
        for u0 in range(0, len(units), A_UNROLL):
            items, dst = [], []
            for r, n in units[u0:u0 + A_UNROLL]:
                cur = pl.ds(n * A_SPAN * dil + r, A_SPAN, stride=dil)
                kk, vv, valid = k_ref[cur, :], v_ref[cur, :], causal
                if n > 0:
                    prev = pl.ds((n - 1) * A_SPAN * dil + r, A_SPAN, stride=dil)
                    kk = jnp.concatenate([k_ref[prev, :], kk], axis=0)
                    vv = jnp.concatenate([v_ref[prev, :], vv], axis=0)
                    valid = band
                items.append((q_ref[cur, :], kk.astype(BF16), vv.astype(BF16), valid))
                dst.append(cur)
            for rows, (o, l) in zip(dst, _pair_attention(items)):
                o_scr[g, rows, :] = o
                l_scr[g, rows, :] = l
    chunk = 2 * A_SPAN
    for c0 in range(0, t, chunk):
        rows = slice(c0, c0 + chunk)
        ls = [l_scr[g, rows, :] for g in range(A_GROUPS)]
        mx = jnp.maximum(jnp.maximum(ls[0], ls[1]), ls[2])
        es = [jnp.exp(l - mx) for l in ls]
        inv = 1.0 / (es[0] + es[1] + es[2])
        out_ref[rows, :] = ((es[0] * inv) * o_scr[0, rows, :] + (es[1] * inv) * o_scr[1, rows, :]
                            + (es[2] * inv) * o_scr[2, rows, :])


def attn_prompt(proj):
    b, t, w = proj.shape
    assert t % (A_SPAN * max(A_DILATIONS)) == 0
    npair = A_HW // LANES
    specs = []
    for g in range(A_GROUPS):
        for c in range(3):
            col = (g * A_GW + c * A_HW) // LANES
            specs.append(pl.BlockSpec((None, t, LANES), functools.partial(
                lambda bi, p, col: (bi, 0, col + p), col=col)))
    kern = functools.partial(_attn_prompt_kernel, t=t)
    o = pl.pallas_call(
        kern,
        grid=(b, npair),
        in_specs=specs,
        out_specs=pl.BlockSpec((None, t, LANES), lambda bi, p: (bi, 0, p)),
        out_shape=jax.ShapeDtypeStruct((b, t, A_HW), F32),
        scratch_shapes=[pltpu.VMEM((A_GROUPS, t, LANES), F32), pltpu.VMEM((A_GROUPS, t, LANES), F32)],
        compiler_params=_cparams("parallel", "parallel"),
        name="attn_prompt",
    )(*([proj] * (3 * A_GROUPS)))
    return o.reshape(b * t, A_HW)


A_TP = 8


def _attn_step_kernel(qkv_ref, c0_ref, c1_ref, c2_ref, o_ref, l_ref, *, ts):
    caches = (c0_ref, c1_ref, c2_ref)
    scale = A_HEAD_DIM ** -0.5
    trow = lax.broadcasted_iota(jnp.int32, (A_TP, 1), 0)
    for g in range(A_GROUPS):
        dil = A_DILATIONS[g]
        lb = caches[g].shape[-1]
        tq = lax.broadcasted_iota(jnp.int32, (A_TP, lb), 0)
        pos = lax.broadcasted_iota(jnp.int32, (A_TP, lb), 1)
        seen = ((pos >= tq) if dil == 1 else (pos % dil == tq)) | (tq >= ts)
        for h in range(A_HEADS):
            q = qkv_ref[3 * g, h]
            knew = qkv_ref[3 * g + 1, h]
            vnew = qkv_ref[3 * g + 2, h]
            s = jnp.where(seen, _bdot(q, caches[g][0, h]) * scale, -jnp.inf)
            s_new = []
            for tn in range(ts):
                sn = jnp.sum(q * knew[tn:tn + 1, :], -1, keepdims=True) * scale
                vis = (trow >= tn) if dil == 1 else (trow == tn)
                s_new.append(jnp.where(vis, sn, -jnp.inf))
            mx = jnp.max(s, -1, keepdims=True)
            for sn in s_new:
                mx = jnp.maximum(mx, sn)
            e = jnp.exp(s - mx)
            den = jnp.sum(e, -1, keepdims=True)
            acc = _bdot_nt(e, caches[g][1, h])
            for tn, sn in enumerate(s_new):
                en = jnp.exp(sn - mx)
                den = den + en
                acc = acc + en * vnew[tn:tn + 1, :]
            o_ref[g, h] = acc * (1.0 / den)
            l_ref[g, h] = jnp.broadcast_to(mx + jnp.log(den), (A_TP, A_HEAD_DIM))


def attn_step(qkv, caches, li, ts):
    b = qkv.shape[0]
    views, specs = [], []
    for g in range(A_GROUPS):
        dil = A_DILATIONS[g]
        lb = caches[g].shape[2]
        assert lb == A_SPAN * dil and (dil == 1 or ts <= dil)
        views.append(caches[g].transpose(0, 1, 3, 4, 5, 2))
        specs.append(pl.BlockSpec((None, None, 2, A_HEADS, A_HEAD_DIM, lb), lambda bi: (li, bi, 0, 0, 0, 0)))
    kern = functools.partial(_attn_step_kernel, ts=ts)
    oshape = (b, A_GROUPS, A_HEADS, A_TP, A_HEAD_DIM)
    ospec = pl.BlockSpec((None, A_GROUPS, A_HEADS, A_TP, A_HEAD_DIM), lambda bi: (bi, 0, 0, 0, 0))
    return pl.pallas_call(
        kern,
        grid=(b,),
        in_specs=[pl.BlockSpec((None, 3 * A_GROUPS, A_HEADS, A_TP, A_HEAD_DIM), lambda bi: (bi, 0, 0, 0, 0))] + specs,
        out_specs=[ospec, ospec],
        out_shape=[jax.ShapeDtypeStruct(oshape, F32)] * 2,
        compiler_params=_cparams("parallel"),
        name="attn_step",
    )(qkv, *views)


def _a_out_kernel(o0, o1, o2, l0, l1, l2, w_ref, x_ref, g_ref, b_ref, out_ref):
    ls = (l0[...], l1[...], l2[...])
    mx = jnp.maximum(jnp.maximum(ls[0], ls[1]), ls[2])
    es = [jnp.exp(l - mx) for l in ls]
    inv = 1.0 / (es[0] + es[1] + es[2])
    o = (es[0] * inv) * o0[...] + (es[1] * inv) * o1[...] + (es[2] * inv) * o2[...]
    y = _bdot(o, w_ref[...])
    out_ref[...] = _layer_norm(DN_ALPHA * x_ref[...] + y, g_ref[...], b_ref[...])


def a_out(os_, ls_, w, x, g, b):
    m, d = x.shape
    tm = min(_row_tile(m), 256)
    ol = [os_] * 3 + [ls_] * 3
    ospecs = [pl.BlockSpec((tm, A_HW), functools.partial(lambda i, gg: (i, gg), gg=gg))
              for gg in range(3)] * 2
    return pl.pallas_call(
        _a_out_kernel,
        grid=(m // tm,),
        in_specs=ospecs + [pl.BlockSpec((A_HW, d), lambda i: (0, 0)),
                           pl.BlockSpec((tm, d), lambda i: (i, 0)),
                           pl.BlockSpec((1, d), lambda i: (0, 0)),
                           pl.BlockSpec((1, d), lambda i: (0, 0))],
        out_specs=pl.BlockSpec((tm, d), lambda i: (i, 0)),
        out_shape=jax.ShapeDtypeStruct((m, d), F32),
        compiler_params=_cparams("parallel"),
        name="a_out",
    )(*ol, w, x, g.reshape(1, d), b.reshape(1, d))


def _split_bf16(a):
    hi = a.astype(BF16)
    lo = (a - hi.astype(F32)).astype(BF16)
    return hi, lo


def _dot3(a, b):
    ah, al = _split_bf16(a)
    bh, bl = _split_bf16(b)
    d = functools.partial(jnp.dot, preferred_element_type=F32)
    return d(ah, bh) + (d(ah, bl) + d(al, bh))


def _gdn_kernel(proj_ref, abt_ref, cbuf_ref, s0_ref, convw_ref, alog_row_ref, dt_row_ref, alog_col_ref,
                dt_col_ref, normw_ref, o_ref, s_ref, xp_ref, *, c, t_valid):
    ci = pl.program_id(1)
    qkv_w = 3 * B_WIDTH

    @pl.when(ci == 0)
    def _():
        s_ref[...] = s0_ref[...]
        xp_ref[8 - (CONV_W - 1):8, :] = cbuf_ref[0]

    xp_ref[8:8 + c, :] = proj_ref[0, :, 0:qkv_w]
    y = None
    for j in range(CONV_W):
        lo = 8 - (CONV_W - 1) + j
        term = xp_ref[lo:lo + c, :] * convw_ref[j:j + 1, :]
        y = term if y is None else y + term
    xp_ref[8 - (CONV_W - 1):8, :] = xp_ref[8 + c - (CONV_W - 1):8 + c, :]
    qkv = y * jax.nn.sigmoid(y)

    ab = proj_ref[0, :, 4 * B_WIDTH:4 * B_WIDTH + LANES]
    trow = lax.broadcasted_iota(jnp.int32, (c, LANES), 0) + ci * c
    live = trow < t_valid
    beta = jnp.where(live, jax.nn.sigmoid(ab), 0.0)
    gate = jnp.where(live, -jnp.exp(alog_row_ref[...]) * _softplus(ab + dt_row_ref[...]), 0.0)
    abt = abt_ref[0, 0]
    tcol = lax.broadcasted_iota(jnp.int32, (B_HEADS, c), 1) + ci * c
    gate_t = jnp.where(tcol < t_valid,
                       -jnp.exp(alog_col_ref[...]) * _softplus(abt[B_HEADS:2 * B_HEADS] + dt_col_ref[...]), 0.0)
    ii = lax.broadcasted_iota(jnp.int32, (c, c), 0)
    jj = lax.broadcasted_iota(jnp.int32, (c, c), 1)
    tri = ii >= jj
    stri = ii > jj
    gcum = _fdot(tri.astype(F32), gate)
    gcum_t = _fdot(gate_t, (ii <= jj).astype(F32))
    eye = (ii == jj).astype(F32)

    heads = range(B_HEADS)
    q_, k_, kb_, vb_, gc_, glast_, a_, qk_ = [], [], [], [], [], [], [], []
    for h in heads:
        qh = qkv[:, h * B_DK:(h + 1) * B_DK]
        kh = qkv[:, B_WIDTH + h * B_DK:B_WIDTH + (h + 1) * B_DK]
        vh = qkv[:, 2 * B_WIDTH + h * B_DV:2 * B_WIDTH + (h + 1) * B_DV]
        qh = qh * lax.rsqrt(jnp.sum(qh * qh, -1, keepdims=True) + 1e-6) * (B_DK ** -0.5)
        kh = kh * lax.rsqrt(jnp.sum(kh * kh, -1, keepdims=True) + 1e-6)
        bh = beta[:, h:h + 1]
        q_.append(qh)
        k_.append(kh)
        kb_.append(kh * bh)
        vb_.append(vh * bh)
        gc_.append(gcum[:, B_HEADS + h:B_HEADS + h + 1])
        glast_.append(gcum[c - 1:c, B_HEADS + h:B_HEADS + h + 1])
    kq_ = [_bdot_nt(jnp.concatenate([kb_[h], q_[h]], axis=0), k_[h]) for h in heads]
    for h in heads:
        dmat = jnp.exp(jnp.where(tri, gc_[h] - gcum_t[h:h + 1, :], -jnp.inf))
        a_.append(jnp.where(stri, kq_[h][0:c] * dmat, 0.0))
        qk_.append(jnp.where(tri, kq_[h][c:2 * c] * dmat, 0.0))
    mpow_ = [-a_[h] for h in heads]
    tinv_ = [eye + mpow_[h] for h in heads]
    for _ in range(int(math.log2(c)) - 1):
        mpow_ = [_dot3(mpow_[h], mpow_[h]) for h in heads]
        tinv_ = [tinv_[h] + _dot3(tinv_[h], mpow_[h]) for h in heads]
    egc_ = [jnp.exp(gc_[h]) for h in heads]
    wu_ = [_bdot(tinv_[h], jnp.concatenate([kb_[h] * egc_[h], vb_[h]], axis=1)) for h in heads]
    s_ = [s_ref[0, h] for h in heads]
    ws_ = [_bdot(jnp.concatenate([wu_[h][:, 0:B_DK], q_[h] * egc_[h]], axis=0), s_[h]) for h in heads]
    vnew_ = [wu_[h][:, B_DK:B_DK + B_DV] - ws_[h][0:c] for h in heads]
    o_ = [ws_[h][c:2 * c] + _bdot(qk_[h], vnew_[h]) for h in heads]
    for h in heads:
        kdec = k_[h] * jnp.exp(glast_[h] - gc_[h])
        upd = lax.dot_general(kdec.astype(BF16), vnew_[h].astype(BF16), (((0,), (0,)), ((), ())),
                              preferred_element_type=F32)
        s_ref[0, h] = s_[h] * jnp.exp(glast_[h]) + upd
    for h in heads:
        zh = proj_ref[0, :, qkv_w + h * B_DV:qkv_w + (h + 1) * B_DV]
        on = o_[h] * lax.rsqrt(jnp.mean(o_[h] * o_[h], -1, keepdims=True) + RMS_EPS) * normw_ref[...]
        o_ref[0, :, h * B_DV:(h + 1) * B_DV] = on * (zh * jax.nn.sigmoid(zh))


def gdn(proj, conv_buf, s0, conv_w, a_log, dt_bias, norm_w, t_valid):
    b, t, w = proj.shape
    c = min(B_CHUNK, t)
    n = t // c
    ab = proj[:, :, 4 * B_WIDTH:4 * B_WIDTH + 2 * B_HEADS]
    abt = ab.reshape(b, n, c, 2 * B_HEADS).transpose(0, 1, 3, 2)
    lane_pad = LANES - 2 * B_HEADS
    alog_row = jnp.pad(a_log.reshape(1, B_HEADS), ((0, 0), (B_HEADS, lane_pad)))
    dt_row = jnp.pad(dt_bias.reshape(1, B_HEADS), ((0, 0), (B_HEADS, lane_pad)))
    kern = functools.partial(_gdn_kernel, c=c, t_valid=t_valid)
    const2 = lambda bi, ci: (0, 0)
    return pl.pallas_call(
        kern,
        grid=(b, n),
        in_specs=[pl.BlockSpec((1, c, w), lambda bi, ci: (bi, ci, 0)),
                  pl.BlockSpec((1, 1, 2 * B_HEADS, c), lambda bi, ci: (bi, ci, 0, 0)),
                  pl.BlockSpec((1, CONV_W - 1, 3 * B_WIDTH), lambda bi, ci: (bi, 0, 0)),
                  pl.BlockSpec((1, B_HEADS, B_DK, B_DV), lambda bi, ci: (bi, 0, 0, 0)),
                  pl.BlockSpec((CONV_W, 3 * B_WIDTH), const2),
                  pl.BlockSpec((1, LANES), const2),
                  pl.BlockSpec((1, LANES), const2),
                  pl.BlockSpec((B_HEADS, 1), const2),
                  pl.BlockSpec((B_HEADS, 1), const2),
                  pl.BlockSpec((1, B_DV), const2)],
        out_specs=[pl.BlockSpec((1, c, B_WIDTH), lambda bi, ci: (bi, ci, 0)),
                   pl.BlockSpec((1, B_HEADS, B_DK, B_DV), lambda bi, ci: (bi, 0, 0, 0))],
        out_shape=[jax.ShapeDtypeStruct((b, t, B_WIDTH), F32),
                   jax.ShapeDtypeStruct((b, B_HEADS, B_DK, B_DV), F32)],
        scratch_shapes=[pltpu.VMEM((8 + c, 3 * B_WIDTH), F32)],
        compiler_params=_cparams("parallel", "arbitrary"),
        name="gdn",
    )(proj, abt, conv_buf, s0, conv_w, alog_row, dt_row, a_log.reshape(B_HEADS, 1),
      dt_bias.reshape(B_HEADS, 1), norm_w.reshape(1, B_DV))


def _rot_half(x, width):
    lane = lax.broadcasted_iota(jnp.int32, x.shape, 1)
    first = (lane % C_ROPE) < C_ROPE // 2
    return jnp.where(first, -pltpu.roll(x, width - C_ROPE // 2, 1), pltpu.roll(x, C_ROPE // 2, 1))


def _mla_prep_kernel(proj_ref, cs_ref, qn_ref, kvn_ref, wuq_ref, wuk_ref, lat_ref, qcat_ref):
    rw = C_HEADS * C_ROPE
    cos_all = cs_ref[:, 0:rw]
    sin_all = cs_ref[:, rw:2 * rw]
    pc = proj_ref[...]
    cq = pc[:, 0:C_Q_LORA]
    cq = cq * lax.rsqrt(jnp.mean(cq * cq, -1, keepdims=True) + RMS_EPS) * qn_ref[...]
    ckv = pc[:, C_Q_LORA:C_Q_LORA + C_KV_LORA]
    ckv = ckv * lax.rsqrt(jnp.mean(ckv * ckv, -1, keepdims=True) + RMS_EPS) * kvn_ref[...]
    lat_ref[:, 0:C_KV_LORA] = ckv
    kr = pc[:, C_Q_LORA + C_KV_LORA:C_Q_LORA + C_KV_LORA + LANES]
    kr = kr * cos_all[:, 0:LANES] + _rot_half(kr, LANES) * sin_all[:, 0:LANES]
    lat_ref[:, C_KV_LORA:C_LAT] = kr[:, 0:C_ROPE]
    q = _bdot(cq, wuq_ref[...])
    nw = C_HEADS * C_NOPE
    qr = q[:, nw:nw + rw]
    qr = qr * cos_all + _rot_half(qr, rw) * sin_all
    for h in range(C_HEADS):
        qcat_ref[h, :, 0:C_KV_LORA] = _bdot(q[:, h * C_NOPE:(h + 1) * C_NOPE], wuk_ref[h])
        qcat_ref[h, :, C_KV_LORA:C_LAT] = qr[:, h * C_ROPE:(h + 1) * C_ROPE]


def mla_prep(proj, cs, q_norm, kv_norm, w_uq_perm, w_uk_t):
    m, w = proj.shape
    tm = min(_row_tile(m), 256)
    nrep = cs.shape[0] // tm
    const2 = lambda i: (0, 0)
    return pl.pallas_call(
        _mla_prep_kernel,
        grid=(m // tm,),
        in_specs=[pl.BlockSpec((tm, w), lambda i: (i, 0)),
                  pl.BlockSpec((tm, cs.shape[1]), lambda i: (i % nrep, 0)),
                  pl.BlockSpec((1, C_Q_LORA), const2),
                  pl.BlockSpec((1, C_KV_LORA), const2),
                  pl.BlockSpec(w_uq_perm.shape, const2),
                  pl.BlockSpec(w_uk_t.shape, lambda i: (0, 0, 0))],
        out_specs=[pl.BlockSpec((tm, C_LAT), lambda i: (i, 0)),
                   pl.BlockSpec((C_HEADS, tm, C_LAT), lambda i: (0, i, 0))],
        out_shape=[jax.ShapeDtypeStruct((m, C_LAT), F32),
                   jax.ShapeDtypeStruct((C_HEADS, m, C_LAT), F32)],
        compiler_params=_cparams("parallel"),
        name="mla_prep",
    )(proj, cs, q_norm.reshape(1, -1), kv_norm.reshape(1, -1), w_uq_perm, w_uk_t)


def _mla_flash_kernel(q_ref, lat_ref, wuv_ref, o_ref, m_ref, l_ref, acc_ref, *, tq, tk):
    qi = pl.program_id(1)
    ki = pl.program_id(2)
    last = (qi * tq + tq - 1) // tk
    rows = C_HEADS * tq

    @pl.when(ki == 0)
    def _():
        m_ref[...] = jnp.full_like(m_ref, -jnp.inf)
        l_ref[...] = jnp.zeros_like(l_ref)
        acc_ref[...] = jnp.zeros_like(acc_ref)

    @pl.when(ki <= last)
    def _():
        lat = lat_ref[0].astype(BF16)
        qpos = qi * tq + lax.broadcasted_iota(jnp.int32, (tq, tk), 0)
        kpos = ki * tk + lax.broadcasted_iota(jnp.int32, (tq, tk), 1)
        visible = (kpos <= qpos)[None]
        ngrp = 2
        hg = C_HEADS // ngrp
        gr = [slice(i * hg * tq, (i + 1) * hg * tq) for i in range(ngrp)]
        ss = [_bdot_nt(q_ref[i * hg:(i + 1) * hg].reshape(hg * tq, C_LAT), lat) * C_SCALE for i in range(ngrp)]
        ss = [jnp.where(visible, s.reshape(hg, tq, tk), -jnp.inf).reshape(hg * tq, tk) for s in ss]
        ps, alphas = [], []
        for i, s in enumerate(ss):
            m_old = m_ref[gr[i], :]
            m_new = jnp.maximum(m_old, jnp.max(s, -1, keepdims=True))
            alpha = jnp.exp(m_old - m_new)
            p = jnp.exp(s - m_new)
            l_ref[gr[i], :] = alpha * l_ref[gr[i], :] + jnp.sum(p, -1, keepdims=True)
            m_ref[gr[i], :] = m_new
            ps.append(p)
            alphas.append(alpha)
        for i in range(ngrp):
            acc_ref[gr[i], :] = alphas[i] * acc_ref[gr[i], :] + _bdot(ps[i], lat[:, 0:C_KV_LORA])

    @pl.when(ki == last)
    def _():
        o = acc_ref[...] * (1.0 / l_ref[...])
        for h in range(C_HEADS):
            o_ref[:, h * C_V:(h + 1) * C_V] = _bdot(o[h * tq:(h + 1) * tq], wuv_ref[h])


def mla_flash(qcat, lat, w_uv_t, b, t):
    tq, tk = 128, 512
    tk = min(tk, t)
    tq = min(tq, t)
    nq, nk = t // tq, t // tk
    kern = functools.partial(_mla_flash_kernel, tq=tq, tk=tk)
    rows = C_HEADS * tq
    return pl.pallas_call(
        kern,
        grid=(b, nq, nk),
        in_specs=[pl.BlockSpec((C_HEADS, tq, C_LAT), lambda bi, qi, ki: (0, bi * nq + qi, 0)),
                  pl.BlockSpec((1, tk, C_LAT),
                               lambda bi, qi, ki: (bi, jnp.minimum(ki, (qi * tq + tq - 1) // tk), 0)),
                  pl.BlockSpec(w_uv_t.shape, lambda bi, qi, ki: (0, 0, 0))],
        out_specs=pl.BlockSpec((tq, C_HEADS * C_V), lambda bi, qi, ki: (bi * nq + qi, 0)),
        out_shape=jax.ShapeDtypeStruct((b * t, C_HEADS * C_V), F32),
        scratch_shapes=[pltpu.VMEM((rows, 1), F32), pltpu.VMEM((rows, 1), F32),
                        pltpu.VMEM((rows, C_KV_LORA), F32)],
        compiler_params=_cparams("parallel", "parallel", "arbitrary"),
        name="mla_flash",
    )(qcat, lat, w_uv_t)


def _mla_step_kernel(pt_ref, q_ref, new_ref, *rest, ts, npg):
    page_refs = rest[:npg]
    wuv_ref, o_ref, m_ref, l_ref, acc_ref = rest[npg:]
    si = pl.program_id(1)
    rows = ts * C_HEADS
    q = q_ref[0]

    @pl.when(si == 0)
    def _():
        new = new_ref[0]
        tq_ = lax.broadcasted_iota(jnp.int32, (rows, 1), 0) // C_HEADS
        s_new = []
        for tn in range(ts):
            sn = jnp.sum(q * new[tn:tn + 1, :], -1, keepdims=True) * C_SCALE
            s_new.append(jnp.where(tq_ >= tn, sn, -jnp.inf))
        mx = s_new[0]
        for sn in s_new[1:]:
            mx = jnp.maximum(mx, sn)
        den = jnp.zeros((rows, 1), F32)
        acc = jnp.zeros((rows, C_KV_LORA), F32)
        for tn in range(ts):
            e = jnp.exp(s_new[tn] - mx)
            den = den + e
            acc = acc + e * new[tn:tn + 1, 0:C_KV_LORA]
        m_ref[...] = mx
        l_ref[...] = den
        acc_ref[...] = acc

    s = jnp.concatenate([_bdot(q, pr[0]) for pr in page_refs], axis=1) * C_SCALE
    m_old = m_ref[...]
    m_new = jnp.maximum(m_old, jnp.max(s, -1, keepdims=True))
    alpha = jnp.exp(m_old - m_new)
    p = jnp.exp(s - m_new)
    l_ref[...] = alpha * l_ref[...] + jnp.sum(p, -1, keepdims=True)
    acc = alpha * acc_ref[...]
    psz = page_refs[0].shape[2]
    for i, pr in enumerate(page_refs):
        acc = acc + _bdot_nt(p[:, i * psz:(i + 1) * psz], pr[0, 0:C_KV_LORA, :])
    acc_ref[...] = acc
    m_ref[...] = m_new

    @pl.when(si == pl.num_programs(1) - 1)
    def _():
        o = acc_ref[...] * (1.0 / l_ref[...])
        full = _bdot(o, wuv_ref[...])
        hrow = lax.broadcasted_iota(jnp.int32, (C_HEADS, C_HEADS * C_V), 0)
        hcol = lax.broadcasted_iota(jnp.int32, (C_HEADS, C_HEADS * C_V), 1) // C_V
        diag = hrow == hcol
        for t in range(ts):
            blk = full[t * C_HEADS:(t + 1) * C_HEADS]
            o_ref[0, t:t + 1, :] = jnp.sum(jnp.where(diag, blk, 0.0), 0, keepdims=True)


def mla_step(qrows, lat_new, pool, page_table, w_uv_flat):
    b, rows, _ = qrows.shape
    ts = rows // C_HEADS
    n_pages = page_table.shape[1]
    psz = pool.shape[1]
    pool = pool.transpose(0, 2, 1)
    npg = 16
    while n_pages % npg:
        npg //= 2
    kern = functools.partial(_mla_step_kernel, ts=ts, npg=npg)

    def page_map(bi, si, pt, k):
        return (pt[bi, si * npg + k], 0, 0)

    grid_spec = pltpu.PrefetchScalarGridSpec(
        num_scalar_prefetch=1,
        grid=(b, n_pages // npg),
        in_specs=[pl.BlockSpec((1, rows, C_LAT), lambda bi, si, pt: (bi, 0, 0)),
                  pl.BlockSpec((1, ts, C_LAT), lambda bi, si, pt: (bi, 0, 0))]
                 + [pl.BlockSpec((1, C_LAT, psz), functools.partial(page_map, k=k)) for k in range(npg)]
                 + [pl.BlockSpec(w_uv_flat.shape, lambda bi, si, pt: (0, 0))],
        out_specs=pl.BlockSpec((1, ts, C_HEADS * C_V), lambda bi, si, pt: (bi, 0, 0)),
        scratch_shapes=[pltpu.VMEM((rows, 1), F32), pltpu.VMEM((rows, 1), F32),
                        pltpu.VMEM((rows, C_KV_LORA), F32)],
    )
    return pl.pallas_call(
        kern,
        grid_spec=grid_spec,
        out_shape=jax.ShapeDtypeStruct((b, ts, C_HEADS * C_V), F32),
        compiler_params=_cparams("parallel", "arbitrary"),
        name="mla_step",
    )(page_table, qrows, lat_new, *([pool] * npg), w_uv_flat)


P_RANKS = P_TOPK + 1


def _pruned_pairs():
    return [(i, j) for i in range(P_RANKS) for j in range(P_RANKS // (i + 1))]


def _extract_top(s, n):
    vals = []
    for _ in range(n):
        mx = jnp.max(s, 0, keepdims=True)
        vals.append(mx)
        s = jnp.where(s == mx, -jnp.inf, s)
    return vals


def _sorting_network(n):
    pairs = []
    p = 1
    while p < n:
        k = p
        while k >= 1:
            for j in range(k % p, n - k, 2 * k):
                for i in range(min(k, n - j - k)):
                    if (i + j) // (2 * p) == (i + j + k) // (2 * p):
                        pairs.append((i + j, i + j + k))
            k //= 2
        p *= 2
    return pairs


def _extract_top_keys(s, n):
    sub = 8
    v = [s[i * sub:(i + 1) * sub] for i in range(N_KEYS // sub)]
    for i, j in _sorting_network(len(v)):
        v[i], v[j] = jnp.maximum(v[i], v[j]), jnp.minimum(v[i], v[j])
    vals = []
    for k in range(n):
        mx = jnp.max(v[0], 0, keepdims=True)
        vals.append(mx)
        hit = v[0] == mx
        live = min(len(v), n - k)
        for d in range(live - 1):
            v[d] = jnp.where(hit, v[d + 1], v[d])
        v[live - 1] = jnp.where(hit, -jnp.inf, v[live - 1])
    return vals


def _candidates(avals, bvals, tn):
    pairs = _pruned_pairs()
    sub = lax.broadcasted_iota(jnp.int32, (8, tn), 0)
    groups = []
    for g0 in range(0, len(pairs), 8):
        grp = pairs[g0:g0 + 8]
        av = jnp.broadcast_to(avals[grp[0][0]], (8, tn))
        bv = jnp.broadcast_to(bvals[grp[0][1]], (8, tn))
        for r in range(1, len(grp)):
            if grp[r][0] != grp[r - 1][0]:
                av = jnp.where(sub >= r, avals[grp[r][0]], av)
            if grp[r][1] != grp[r - 1][1]:
                bv = jnp.where(sub >= r, bvals[grp[r][1]], bv)
        cand = av + bv
        if len(grp) < 8:
            cand = jnp.where(sub < len(grp), cand, -jnp.inf)
        groups.append(cand)
    return jnp.concatenate(groups, axis=0)


def _peer_topk_kernel(q_ref, keys_ref, s1_ref, s2_ref, thr_ref):
    tn = q_ref.shape[0]
    half = P_QDIM // 2
    sub = lax.broadcasted_iota(jnp.int32, (P_HEADS, tn), 0)
    thr = jnp.zeros((P_HEADS, tn), F32)
    for h in range(P_HEADS):
        sc = []
        for c in range(2):
            qs = q_ref[:, h * P_QDIM + c * half:h * P_QDIM + (c + 1) * half]
            s = _bdot_nt(keys_ref[h, c], qs)
            sc.append(s - jnp.max(s, 0, keepdims=True))
        a = _extract_top_keys(sc[0], P_RANKS)
        bvals = _extract_top_keys(sc[1], P_RANKS)
        top = _extract_top(_candidates(a, bvals, tn), P_RANKS)
        z = jnp.exp(top[0])
        for v in top[1:P_TOPK]:
            z = z + jnp.exp(v)
        logz = jnp.log(z)
        s1_ref[h] = (sc[0] - logz) * LOG2E
        s2_ref[h] = sc[1] * LOG2E
        mid = 0.5 * (top[P_TOPK - 1] + top[P_TOPK])
        thr = jnp.where(sub == h, (mid - logz) * LOG2E, thr)
    thr_ref[...] = thr


def peer_topk(q, keys_bf16):
    m = q.shape[0]
    tn = LANES
    return pl.pallas_call(
        _peer_topk_kernel,
        grid=(m // tn,),
        in_specs=[pl.BlockSpec((tn, q.shape[1]), lambda i: (i, 0)),
                  pl.BlockSpec(keys_bf16.shape, lambda i: (0, 0, 0, 0))],
        out_specs=[pl.BlockSpec((P_HEADS, N_KEYS, tn), lambda i: (0, 0, i)),
                   pl.BlockSpec((P_HEADS, N_KEYS, tn), lambda i: (0, 0, i)),
                   pl.BlockSpec((P_HEADS, tn), lambda i: (0, i))],
        out_shape=[jax.ShapeDtypeStruct((P_HEADS, N_KEYS, m), F32),
                   jax.ShapeDtypeStruct((P_HEADS, N_KEYS, m), F32),
                   jax.ShapeDtypeStruct((P_HEADS, m), F32)],
        compiler_params=_cparams("parallel"),
        name="peer_topk",
    )(q, keys_bf16)


def _gelu(v):
    return 0.5 * v * (1.0 + lax.erf(v * (2.0 ** -0.5)))


PEER_I1 = 8
PEER_TE = PEER_I1 * N_KEYS
PEER_TN = 256


def _expert_hidden(u_ref, xb_ref, rows=slice(None)):
    return lax.dot_general(u_ref[rows, :], xb_ref[...], (((1,), (1,)), ((), ())), preferred_element_type=F32)


def _gate_stage(s1_ref, s2_ref, thr_ref, ht_ref, at_ref, i1_base, ci, between=()):
    sub = 8
    cols = slice(ci * LANES, (ci + 1) * LANES)
    s1_rows = [s1_ref[h, pl.ds(i1_base, PEER_I1), cols] for h in range(P_HEADS)]
    thr = thr_ref[:, cols]
    thr_b = [jnp.broadcast_to(thr[h:h + 1, :], (sub, LANES)) for h in range(P_HEADS)]
    between = list(between)
    every = PEER_I1 // max(len(between), 1)
    for j in range(PEER_I1):
        if between and j % every == 0:
            between.pop(0)()
        gsum = [jnp.zeros((sub, LANES), F32)] * (N_KEYS // sub)
        for h in range(P_HEADS):
            row = jnp.broadcast_to(s1_rows[h][j:j + 1, :], (sub, LANES))
            for v in range(N_KEYS // sub):
                ssum = row + s2_ref[h, v * sub:(v + 1) * sub, cols]
                gsum[v] = gsum[v] + jnp.exp2(jnp.where(ssum >= thr_b[h], ssum, NEG_BIG))
        rows = slice(j * N_KEYS, (j + 1) * N_KEYS)
        act = _gelu(ht_ref[rows, cols]) * jnp.concatenate(gsum, axis=0)
        at_ref[rows, cols] = act.astype(BF16)


def _peer_dense_kernel(flag_ref, x_ref, s1_ref, s2_ref, thr_ref, u0_ref, ua_ref, ub_ref, va_ref, vb_ref, vl_ref,
                       g_ref, b_ref, o_ref, xb_ref, ht0_ref, ht1_ref, at0_ref, at1_ref, acc_ref):
    si = pl.program_id(1)
    tm = x_ref.shape[0]
    nc = tm // LANES
    vdot = functools.partial(jnp.dot, preferred_element_type=F32)

    @pl.when(si == 0)
    def _():
        xb_ref[...] = x_ref[...].astype(BF16)
        acc_ref[...] = jnp.zeros_like(acc_ref)
        at1_ref[...] = jnp.zeros_like(at1_ref)
        ht0_ref[...] = _expert_hidden(u0_ref, xb_ref)

    te, d = ua_ref.shape

    def tile(u_next_ref, ht_next_ref, v_prev_ref, at_prev_ref, ht_ref, at_ref, i1_base):
        for ci in range(nc):
            er = slice(ci * (te // nc), (ci + 1) * (te // nc))
            dr = slice(ci * (d // nc), (ci + 1) * (d // nc))
            pieces = []
            tn = min(PEER_TN, tm)
            for n0 in range(0, tm, tn):
                tok = slice(n0, n0 + tn)

                def hidden(tok=tok, er=er):
                    ht_next_ref[er, tok] = lax.dot_general(
                        u_next_ref[er, :], xb_ref[tok, :], (((1,), (1,)), ((), ())), preferred_element_type=F32)

                def value(tok=tok, dr=dr):
                    acc_ref[dr, tok] += vdot(v_prev_ref[dr, :], at_prev_ref[:, tok])

                pieces += [hidden, value]

            @pl.when(flag_ref[0] > 0)
            def _(ci=ci, pieces=pieces):
                _gate_stage(s1_ref, s2_ref, thr_ref, ht_ref, at_ref, i1_base, ci, pieces)

    i1_even = pl.multiple_of(si * (2 * PEER_I1), PEER_I1)

    tile(ua_ref, ht1_ref, va_ref, at1_ref, ht0_ref, at0_ref, i1_even)
    tile(ub_ref, ht0_ref, vb_ref, at0_ref, ht1_ref, at1_ref, i1_even + PEER_I1)

    @pl.when(si == pl.num_programs(1) - 1)
    def _():
        y = acc_ref[...] + vdot(vl_ref[...], at1_ref[...])
        o_ref[...] = _layer_norm(DN_ALPHA * x_ref[...] + y.T, g_ref[...], b_ref[...])


def peer_dense(x, s1, s2, thr, u_bf16, vt_bf16, g, b):
    m, d = x.shape
    nt = u_bf16.shape[0] // PEER_TE
    assert nt % 2 == 0
    tm = min(_row_tile(m), 512)
    last = nt - 1
    u_spec = lambda f: pl.BlockSpec((PEER_TE, d), lambda i, s, fl: (f(s), 0))
    v_spec = lambda f: pl.BlockSpec((d, PEER_TE), lambda i, s, fl: (0, f(s)))
    grid_spec = pltpu.PrefetchScalarGridSpec(
        num_scalar_prefetch=1,
        grid=(m // tm, nt // 2),
        in_specs=[pl.BlockSpec((tm, d), lambda i, s, fl: (i, 0)),
                  pl.BlockSpec((P_HEADS, N_KEYS, tm), lambda i, s, fl: (0, 0, i)),
                  pl.BlockSpec((P_HEADS, N_KEYS, tm), lambda i, s, fl: (0, 0, i)),
                  pl.BlockSpec((P_HEADS, tm), lambda i, s, fl: (0, i)),
                  u_spec(lambda s: 0),
                  u_spec(lambda s: 2 * s + 1),
                  u_spec(lambda s: jnp.minimum(2 * s + 2, last)),
                  v_spec(lambda s: jnp.maximum(2 * s - 1, 0)),
                  v_spec(lambda s: 2 * s),
                  v_spec(lambda s: last),
                  pl.BlockSpec((1, d), lambda i, s, fl: (0, 0)),
                  pl.BlockSpec((1, d), lambda i, s, fl: (0, 0))],
        out_specs=pl.BlockSpec((tm, d), lambda i, s, fl: (i, 0)),
        scratch_shapes=[pltpu.VMEM((tm, d), BF16),
                        pltpu.VMEM((PEER_TE, tm), F32), pltpu.VMEM((PEER_TE, tm), F32),
                        pltpu.VMEM((PEER_TE, tm), BF16), pltpu.VMEM((PEER_TE, tm), BF16),
                        pltpu.VMEM((d, tm), F32)])
    return pl.pallas_call(
        _peer_dense_kernel,
        grid_spec=grid_spec,
        out_shape=jax.ShapeDtypeStruct((m, d), F32),
        compiler_params=_cparams("parallel", "arbitrary"),
        name="peer_dense",
    )(jnp.ones((1,), jnp.int32), x, s1, s2, thr, u_bf16, u_bf16, u_bf16, vt_bf16, vt_bf16, vt_bf16,
      g.reshape(1, d), b.reshape(1, d))


def peer_layer(x, wq_bf16, keys_bf16, u_bf16, vt_bf16, g, b):
    q = matmul(x, wq_bf16)
    s1, s2, thr = peer_topk(q, keys_bf16)
    return peer_dense(x, s1, s2, thr, u_bf16, vt_bf16, g, b)


def _pad_cols(w, mult=LANES):
    n = w.shape[-1]
    return jnp.pad(w, ((0, 0),) * (w.ndim - 1) + ((0, _round_up(n, mult) - n),))


def _rope_table(pos):
    half = C_ROPE // 2
    inv = ROPE_THETA ** (-jnp.arange(half, dtype=F32) / half)
    ang = pos.astype(F32)[:, None] * inv[None, :]
    cos = jnp.tile(jnp.cos(ang), (1, 2 * C_HEADS))
    sin = jnp.tile(jnp.sin(ang), (1, 2 * C_HEADS))
    return jnp.concatenate([cos, sin], axis=1)


def _mixer_a(x, bsz, t, w_in, w_out, g, b, bufs, li):
    m, d = x.shape
    proj = matmul(x, w_in.astype(BF16))
    p3 = proj.reshape(bsz, t, A_GROUPS * A_GW)
    rows = []
    if bufs is None:
        for gi in range(A_GROUPS):
            keep = min(A_WINDOWS[gi], t)
            rows.append(p3[:, t - keep:, gi * A_GW + A_HW:(gi + 1) * A_GW]
                        .reshape(bsz, keep, 2, A_HEADS, A_HEAD_DIM))
        x1 = matmul_res_ln(attn_prompt(p3), w_out.astype(BF16), x, g, b)
    else:
        qkv = p3.reshape(bsz, t, 3 * A_GROUPS, A_HEADS, A_HEAD_DIM)
        for gi in range(A_GROUPS):
            rows.append(qkv[:, :, 3 * gi + 1:3 * gi + 3])
        qkv_t = jnp.pad(qkv.transpose(0, 2, 3, 1, 4), ((0, 0), (0, 0), (0, 0), (0, A_TP - t), (0, 0)))
        o, l = attn_step(qkv_t, bufs, li, t)
        tok_major = lambda a: a[:, :, :, :t].transpose(0, 3, 1, 2, 4).reshape(m, A_GROUPS * A_HW)
        x1 = a_out(tok_major(o), tok_major(l), w_out.astype(BF16), x, g, b)
    return x1, rows


def _mixer_b(x, bsz, t, w_in, conv_w, a_log, dt_bias, norm_w, w_out, g, b, conv_buf, s0):
    proj = matmul(x, _pad_cols(w_in).astype(BF16))
    p3 = proj.reshape(bsz, t, proj.shape[1])
    if conv_buf is None:
        conv_buf = jnp.zeros((bsz, CONV_W - 1, 3 * B_WIDTH), F32)
    if s0 is None:
        s0 = jnp.zeros((bsz, B_HEADS, B_DK, B_DV), F32)
    new_buf = jnp.concatenate([conv_buf, p3[:, :, :3 * B_WIDTH]], axis=1)[:, t:]
    c = min(B_CHUNK, _round_up(t, 8))
    tp = _round_up(t, c)
    if tp != t:
        p3 = jnp.pad(p3, ((0, 0), (0, tp - t), (0, 0)))
    o, s_fin = gdn(p3, conv_buf, s0, conv_w, a_log, dt_bias, norm_w, t)
    o = o[:, :t].reshape(bsz * t, B_WIDTH)
    x1 = matmul_res_ln(o, w_out.astype(BF16), x, g, b)
    return x1, new_buf, s_fin


def _mixer_c(x, bsz, t, pos, w_in, q_norm, kv_norm, w_uq, w_uk, w_uv, w_out, g, b, pool, page_table):
    m = x.shape[0]
    proj = matmul(x, _pad_cols(w_in).astype(BF16))
    wq3 = w_uq.reshape(C_Q_LORA, C_HEADS, C_NOPE + C_ROPE)
    w_uq_perm = jnp.concatenate([wq3[:, :, :C_NOPE].reshape(C_Q_LORA, -1),
                                 wq3[:, :, C_NOPE:].reshape(C_Q_LORA, -1)], axis=1).astype(BF16)
    w_uk_t = w_uk.transpose(1, 2, 0).astype(BF16)
    tm = min(_row_tile(m), 256)
    table = _rope_table(pos)
    reps = max(tm // t, 1)
    cs = jnp.tile(table, (reps, 1))
    lat_new, qcat = mla_prep(proj, cs, q_norm, kv_norm, w_uq_perm, w_uk_t)
    if pool is None:
        w_uv_t = w_uv.transpose(1, 0, 2).astype(BF16)
        o = mla_flash(qcat, lat_new.reshape(bsz, t, C_LAT), w_uv_t, bsz, t)
    else:
        qrows = qcat.reshape(C_HEADS, bsz, t, C_LAT).transpose(1, 2, 0, 3).reshape(bsz, t * C_HEADS, C_LAT)
        w_uv_flat = w_uv.reshape(C_KV_LORA, C_HEADS * C_V).astype(BF16)
        o = mla_step(qrows, lat_new.reshape(bsz, t, C_LAT), pool, page_table, w_uv_flat)
        o = o.reshape(m, C_HEADS * C_V)
    x1 = matmul_res_ln(o, w_out.astype(BF16), x, g, b)
    return x1, lat_new.reshape(bsz, t, C_LAT)


def _run_trunk(x3, pos, prm, peer, cache):
    bsz, t, d = x3.shape
    x = x3.reshape(bsz * t, d)
    new_a = [[] for _ in range(A_GROUPS)]
    new_conv, new_rec, new_lat = [], [], []
    for i in range(DEPTH):
        kind, li = i % N_MIXERS, i // N_MIXERS
        g1, b1 = prm['ln1_g'][i], prm['ln1_b'][i]
        if kind == 0:
            bufs = None if cache is None else cache['a_win']
            x, rows = _mixer_a(x, bsz, t, prm['w_a_in'][li], prm['w_a_out'][li], g1, b1, bufs, li)
            for gi in range(A_GROUPS):
                new_a[gi].append(rows[gi])
        elif kind == 1:
            conv_buf = None if cache is None else cache['b_conv'][li]
            s0 = None if cache is None else cache['b_rec'][li]
            x, cb, st = _mixer_b(x, bsz, t, prm['w_b_in'][li], prm['b_conv_w'][li], prm['b_a_log'][li],
                                 prm['b_dt_bias'][li], prm['b_norm_w'][li], prm['w_b_out'][li], g1, b1,
                                 conv_buf, s0)
            new_conv.append(cb)
            new_rec.append(st)
        else:
            pool = None if cache is None else cache['c_pool'][li]
            pt = None if cache is None else cache['page_table']
            x, lat = _mixer_c(x, bsz, t, pos, prm['w_c_in'][li], prm['c_q_norm'][li], prm['c_kv_norm'][li],
                              prm['w_c_uq'][li], prm['w_c_uk'][li], prm['w_c_uv'][li], prm['w_c_out'][li],
                              g1, b1, pool, pt)
            new_lat.append(lat)
        wq, keys, u, vt = peer[i]
        x = peer_layer(x, wq, keys, u, vt, prm['ln2_g'][i], prm['ln2_b'][i])
    return (x.reshape(bsz, t, d), [jnp.stack(r) for r in new_a], jnp.stack(new_conv), jnp.stack(new_rec),
            jnp.stack(new_lat))


def kernel(x_prompt, x_sample, cache_a_win0, cache_a_win1, cache_a_win2, state_b_conv, state_b_rec,
           cache_c_latent, page_table, w_a_in, w_a_out, w_b_in, b_conv_w, b_a_log, b_dt_bias, b_norm_w,
           w_b_out, w_c_in, c_q_norm, c_kv_norm, w_c_uq, w_c_uk, w_c_uv, w_c_out, ln1_g, ln1_b, ln2_g,
           ln2_b, peer_wq, peer_keys, peer_u, peer_v):
    prm = dict(w_a_in=w_a_in, w_a_out=w_a_out, w_b_in=w_b_in, b_conv_w=b_conv_w, b_a_log=b_a_log,
               b_dt_bias=b_dt_bias, b_norm_w=b_norm_w, w_b_out=w_b_out, w_c_in=w_c_in, c_q_norm=c_q_norm,
               c_kv_norm=c_kv_norm, w_c_uq=w_c_uq, w_c_uk=w_c_uk, w_c_uv=w_c_uv, w_c_out=w_c_out,
               ln1_g=ln1_g, ln1_b=ln1_b, ln2_g=ln2_g, ln2_b=ln2_b)
    peer = [(peer_wq[i].astype(BF16), peer_keys[i].astype(BF16), peer_u[i].astype(BF16),
             peer_v[i].astype(BF16).T) for i in range(DEPTH)]
    y_p, a_p, b_conv_p, b_rec_p, c_lat_p = _run_trunk(x_prompt, jnp.arange(x_prompt.shape[1]), prm, peer, None)
    past_len = page_table.shape[1] * cache_c_latent.shape[2]
    cache = dict(a_win=(cache_a_win0, cache_a_win1, cache_a_win2), b_conv=state_b_conv, b_rec=state_b_rec,
                 c_pool=cache_c_latent, page_table=page_table)
    y_s, a_s, b_conv_s, b_rec_s, c_lat_s = _run_trunk(
        x_sample, past_len + jnp.arange(x_sample.shape[1]), prm, peer, cache)
    return (y_p, y_s, a_p[0], a_p[1], a_p[2], a_s[0], a_s[1], a_s[2],
            b_conv_p, b_rec_p, b_conv_s, b_rec_s, c_lat_p, c_lat_s)
```
